```python
import math
import jax
import jax.numpy as jnp
from jax import lax
import numpy as np


D_MODEL = 4096
BATCH = 2
SEQ = 8192
DEPTH = 2

N_BRANCH = 4
D_BRANCH = D_MODEL // 4
EPS = 1e-6
LRU_BLOCKS = 8
LRU_BLOCK = D_BRANCH // LRU_BLOCKS
LRU_CONV = 4
LRU_C = 8.0
POOL_WINDOWS = (2, 4, 8, 16)
POOL_GROUP = D_BRANCH // len(POOL_WINDOWS)
DA_HEAD_DIM = 64
DA_HEADS = D_BRANCH // (2 * DA_HEAD_DIM)
DA_V_DIM = 2 * DA_HEAD_DIM
ATTN_SCALE = 1.0 / math.sqrt(DA_HEAD_DIM)
ROPE_DIM = DA_HEAD_DIM // 4
ROPE_THETA = 500000.0
Q_BLOCK = 128
CV_KERNEL = 31
N_GROUPS = 4
EXPERTS_PER_GROUP = 8
N_EXPERTS = N_GROUPS * EXPERTS_PER_GROUP
TOP_K = 2
D_EXPERT = 512
MAX_POS_OFFSET = 4096
OFF_LRU_X = 0
OFF_LRU_GATE = D_BRANCH
OFF_POOL = 2 * D_BRANCH
OFF_Q = 3 * D_BRANCH
OFF_K = 4 * D_BRANCH
OFF_V = 5 * D_BRANCH
OFF_GLU = 6 * D_BRANCH
N_MIX = 8 * D_BRANCH
N_IN = N_MIX + N_BRANCH * D_MODEL

kernel_name = 'hybrid_rglru_pool_diffattn_conformer_hiermoe'


def _rmsnorm(x, g):
    xf = x.astype(jnp.float32)
    y = xf * lax.rsqrt(jnp.mean(xf * xf, axis=-1, keepdims=True) + EPS)
    return (y * g.astype(jnp.float32)).astype(x.dtype)


def _layernorm(x, g, b):
    xf = x.astype(jnp.float32)
    mu = jnp.mean(xf, axis=-1, keepdims=True)
    var = jnp.mean(jnp.square(xf - mu), axis=-1, keepdims=True)
    y = (xf - mu) * lax.rsqrt(var + EPS)
    return (y * g.astype(jnp.float32) + b.astype(jnp.float32)).astype(x.dtype)


def _causal_dwconv(x, w, b):
    width, ch = w.shape
    y = lax.conv_general_dilated(x, w[:, None, :].astype(x.dtype), window_strides=(1,),
                                 padding=[(width - 1, 0)],
                                 dimension_numbers=('NWC', 'WIO', 'NWC'),
                                 feature_group_count=ch)
    return y + b.astype(x.dtype)


def _partial_rope(x, cos, sin):
    half = ROPE_DIM // 2
    x1 = x[..., :half].astype(jnp.float32)
    x2 = x[..., half:ROPE_DIM].astype(jnp.float32)
    rot = jnp.concatenate([x1 * cos - x2 * sin, x2 * cos + x1 * sin], axis=-1).astype(x.dtype)
    return jnp.concatenate([rot, x[..., ROPE_DIM:]], axis=-1)


def _rg_lru(xa, w_r, b_r, w_i, b_i, lam):
    bsz, s, _ = xa.shape
    xf = xa.astype(jnp.float32)
    xh = xf.reshape(bsz, s, LRU_BLOCKS, LRU_BLOCK)
    r = jax.nn.sigmoid(jnp.einsum('bshi,hij->bshj', xh, w_r.astype(jnp.float32)).reshape(bsz, s, D_BRANCH) + b_r.astype(jnp.float32))
    i = jax.nn.sigmoid(jnp.einsum('bshi,hij->bshj', xh, w_i.astype(jnp.float32)).reshape(bsz, s, D_BRANCH) + b_i.astype(jnp.float32))
    log_a = -LRU_C * r * jax.nn.softplus(-lam.astype(jnp.float32))
    a = jnp.exp(log_a)
    u = jnp.sqrt(-jnp.expm1(2.0 * log_a)) * (i * xf)

    def combine(left, right):
        a_l, u_l = left
        a_r, u_r = right
        return a_l * a_r, a_r * u_l + u_r

    _, h = lax.associative_scan(combine, (a, u), axis=1)
    return h.astype(xa.dtype)


def _pool_mixer(xp, w_pool, b_pool, scale):
    bsz, s, _ = xp.shape
    xf = xp.astype(jnp.float32)
    csum = jnp.pad(jnp.cumsum(xf, axis=1), ((0, 0), (1, 0), (0, 0)))
    t1 = jnp.arange(1, s + 1, dtype=jnp.float32)[None, :, None]
    diffs = []
    for gi, win in enumerate(POOL_WINDOWS):
        sl = slice(gi * POOL_GROUP, (gi + 1) * POOL_GROUP)
        cg = csum[..., sl]
        upper = cg[:, 1:]
        lower = jnp.pad(cg, ((0, 0), (win - 1, 0), (0, 0)))[:, :s]
        mean = (upper - lower) / jnp.minimum(t1, float(win))
        diffs.append(mean - xf[..., sl])
    d = jnp.stack(diffs, axis=2)
    y = jnp.einsum('bsgi,gij->bsgj', d, w_pool.astype(jnp.float32)) + b_pool.astype(jnp.float32).reshape(len(POOL_WINDOWS), POOL_GROUP)
    return (y.reshape(bsz, s, D_BRANCH) * scale.astype(jnp.float32)).astype(xp.dtype)


def _diff_attention(q, k, v, cos, sin, q_g, k_g, lq1, lk1, lq2, lk2, sub_g, lambda_init):
    bsz, s = q.shape[:2]
    q = _partial_rope(_rmsnorm(q, q_g), cos, sin)
    k = _partial_rope(_rmsnorm(k, k_g), cos, sin)
    f32 = jnp.float32
    lam = (jnp.exp(jnp.sum(lq1.astype(f32) * lk1.astype(f32)))
           - jnp.exp(jnp.sum(lq2.astype(f32) * lk2.astype(f32))) + lambda_init)
    n_blk = s // Q_BLOCK
    qb = q.reshape(bsz, n_blk, Q_BLOCK, DA_HEADS, 2, DA_HEAD_DIM).transpose(1, 0, 2, 3, 4, 5)
    kpos = jnp.arange(s)

    def block(args):
        q_blk, blk = args
        sc = jnp.einsum('bqhcd,bkhcd->bhcqk', q_blk, k, preferred_element_type=f32) * ATTN_SCALE
        qpos = blk * Q_BLOCK + jnp.arange(Q_BLOCK)
        sc = jnp.where(kpos[None, :] <= qpos[:, None], sc, -jnp.inf)
        p = jax.nn.softmax(sc, axis=-1)
        amap = p[:, :, 0] - lam * p[:, :, 1]
        return jnp.einsum('bhqk,bkhd->bqhd', amap.astype(v.dtype), v)

    o = lax.map(block, (qb, jnp.arange(n_blk)))
    o = o.transpose(1, 0, 2, 3, 4).reshape(bsz, s, DA_HEADS, DA_V_DIM)
    o = _rmsnorm(o, sub_g) * (1.0 - lambda_init)
    return o.reshape(bsz, s, D_BRANCH)


def _conformer_conv(xg, dw_w, dw_b, ln_g, ln_b):
    val, gt = jnp.split(xg, 2, axis=-1)
    u = val * jax.nn.sigmoid(gt)
    u = _causal_dwconv(u, dw_w, dw_b)
    return jax.nn.silu(_layernorm(u, ln_g, ln_b))


def _hier_moe(h, wg, bg, we, be, w1, w3, w2):
    bsz, s, d = h.shape
    tok = h.reshape(bsz * s, d)
    g_prob = jax.nn.softmax((tok @ wg + bg).astype(jnp.float32), axis=-1)
    g_top, g_idx = lax.top_k(g_prob, 1)
    e_logits = (tok @ we + be).astype(jnp.float32).reshape(-1, N_GROUPS, EXPERTS_PER_GROUP)
    e_in = jnp.take_along_axis(e_logits, g_idx[:, :, None], axis=1)[:, 0]
    e_val, e_idx = lax.top_k(e_in, TOP_K)
    w_tok = g_top * jax.nn.softmax(e_val, axis=-1)
    expert_id = g_idx * EXPERTS_PER_GROUP + e_idx
    comb = jnp.einsum('nk,nke->ne', w_tok, jax.nn.one_hot(expert_id, N_EXPERTS, dtype=jnp.float32)).astype(h.dtype)
    out = jnp.zeros_like(tok)
    for e in range(N_EXPERTS):
        hid = jax.nn.silu(tok @ w1[e]) * (tok @ w3[e])
        out = out + comb[:, e:e + 1] * (hid @ w2[e])
    return out.reshape(bsz, s, d)


def setup_inputs(seed: int = 0) -> dict:
    key = jax.random.key(seed)
    ks = jax.random.split(key, 40)
    L = DEPTH
    f32 = jnp.float32

    def nrm(i, shape, scale):
        return jax.random.normal(ks[i], shape, f32) * scale

    def gain(i, shape):
        return 1.0 + 0.1 * jax.random.normal(ks[i], shape, f32)

    x = nrm(0, (BATCH, SEQ, D_MODEL), 1.0)
    c = nrm(1, (BATCH, D_MODEL), 1.0)
    positions = (jax.random.randint(ks[2], (BATCH, 1), 0, MAX_POS_OFFSET, jnp.int32)
                 + jnp.arange(SEQ, dtype=jnp.int32)[None, :])
    a0 = jax.random.uniform(ks[16], (L, D_BRANCH), f32, minval=0.9, maxval=0.999)
    return {
        'x': x,
        'c': c,
        'positions': positions,
        'ada_w': nrm(3, (L, D_MODEL, 6 * D_MODEL), 0.5 * D_MODEL ** -0.5),
        'ada_b': nrm(4, (L, 6 * D_MODEL), 0.02),
        'norm1_g': gain(5, (L, D_MODEL)),
        'norm2_g': gain(6, (L, D_MODEL)),
        'w_in': nrm(7, (L, D_MODEL, N_IN), D_MODEL ** -0.5),
        'b_gate': nrm(8, (L, N_BRANCH, D_MODEL), 0.1),
        'lru_conv_w': nrm(9, (L, LRU_CONV, D_BRANCH), LRU_CONV ** -0.5),
        'lru_conv_b': nrm(10, (L, D_BRANCH), 0.02),
        'lru_wr': nrm(11, (L, LRU_BLOCKS, LRU_BLOCK, LRU_BLOCK), LRU_BLOCK ** -0.5),
        'lru_br': nrm(12, (L, D_BRANCH), 0.1),
        'lru_wi': nrm(13, (L, LRU_BLOCKS, LRU_BLOCK, LRU_BLOCK), LRU_BLOCK ** -0.5),
        'lru_bi': nrm(14, (L, D_BRANCH), 0.1),
        'lru_lambda': jnp.log(a0) - jnp.log1p(-a0),
        'pool_w': nrm(17, (L, len(POOL_WINDOWS), POOL_GROUP, POOL_GROUP), POOL_GROUP ** -0.5),
        'pool_b': nrm(18, (L, D_BRANCH), 0.02),
        'pool_scale': gain(19, (L, D_BRANCH)),
        'q_norm_g': gain(20, (L, DA_HEAD_DIM)),
        'k_norm_g': gain(21, (L, DA_HEAD_DIM)),
        'lam_q1': nrm(22, (L, DA_HEAD_DIM), 0.1),
        'lam_k1': nrm(23, (L, DA_HEAD_DIM), 0.1),
        'lam_q2': nrm(24, (L, DA_HEAD_DIM), 0.1),
        'lam_k2': nrm(25, (L, DA_HEAD_DIM), 0.1),
        'subln_g': gain(26, (L, DA_V_DIM)),
        'cv_dw_w': nrm(27, (L, CV_KERNEL, D_BRANCH), CV_KERNEL ** -0.5),
        'cv_dw_b': nrm(28, (L, D_BRANCH), 0.02),
        'cv_ln_g': gain(29, (L, D_BRANCH)),
        'cv_ln_b': nrm(30, (L, D_BRANCH), 0.02),
        'w_branch': nrm(31, (L, N_BRANCH, D_BRANCH, D_MODEL), D_BRANCH ** -0.5),
        'w_out': nrm(32, (L, D_MODEL, D_MODEL), D_MODEL ** -0.5),
        'router_g_w': nrm(33, (L, D_MODEL, N_GROUPS), D_MODEL ** -0.5),
        'router_g_b': nrm(34, (L, N_GROUPS), 0.01),
        'router_e_w': nrm(35, (L, D_MODEL, N_EXPERTS), D_MODEL ** -0.5),
        'router_e_b': nrm(36, (L, N_EXPERTS), 0.01),
        'moe_w1': nrm(37, (L, N_EXPERTS, D_MODEL, D_EXPERT), D_MODEL ** -0.5),
        'moe_w3': nrm(38, (L, N_EXPERTS, D_MODEL, D_EXPERT), D_MODEL ** -0.5),
        'moe_w2': nrm(39, (L, N_EXPERTS, D_EXPERT, D_MODEL), D_EXPERT ** -0.5),
    }


def reference(x, c, positions, ada_w, ada_b, norm1_g, norm2_g, w_in, b_gate,
              lru_conv_w, lru_conv_b, lru_wr, lru_br, lru_wi, lru_bi, lru_lambda,
              pool_w, pool_b, pool_scale, q_norm_g, k_norm_g, lam_q1, lam_k1, lam_q2, lam_k2,
              subln_g, cv_dw_w, cv_dw_b, cv_ln_g, cv_ln_b, w_branch, w_out,
              router_g_w, router_g_b, router_e_w, router_e_b, moe_w1, moe_w3, moe_w2):
    bsz, s, _ = x.shape
    inv_freq = ROPE_THETA ** (-jnp.arange(0, ROPE_DIM, 2, dtype=jnp.float32) / ROPE_DIM)
    ang = positions.astype(jnp.float32)[..., None] * inv_freq
    cos = jnp.cos(ang)[:, :, None, None, :]
    sin = jnp.sin(ang)[:, :, None, None, :]
    c_act = jax.nn.silu(c)
    for l in range(DEPTH):
        mod = c_act @ ada_w[l] + ada_b[l]
        sh1, sc1, g1, sh2, sc2, g2 = jnp.split(mod[:, None, :], 6, axis=-1)
        h = _rmsnorm(x, norm1_g[l]) * (1.0 + sc1) + sh1
        wl = w_in[l]
        proj = h @ wl[:, :N_MIX]
        xa = _causal_dwconv(proj[..., OFF_LRU_X:OFF_LRU_GATE], lru_conv_w[l], lru_conv_b[l])
        y_a = jax.nn.gelu(proj[..., OFF_LRU_GATE:OFF_POOL]) * _rg_lru(
            xa, lru_wr[l], lru_br[l], lru_wi[l], lru_bi[l], lru_lambda[l])
        y_b = _pool_mixer(proj[..., OFF_POOL:OFF_Q], pool_w[l], pool_b[l], pool_scale[l])
        q = proj[..., OFF_Q:OFF_K].reshape(bsz, s, DA_HEADS, 2, DA_HEAD_DIM)
        k = proj[..., OFF_K:OFF_V].reshape(bsz, s, DA_HEADS, 2, DA_HEAD_DIM)
        v = proj[..., OFF_V:OFF_GLU].reshape(bsz, s, DA_HEADS, DA_V_DIM)
        lambda_init = 0.8 - 0.6 * math.exp(-0.3 * l)
        y_c = _diff_attention(q, k, v, cos, sin, q_norm_g[l], k_norm_g[l], lam_q1[l], lam_k1[l],
                              lam_q2[l], lam_k2[l], subln_g[l], lambda_init)
        y_d = _conformer_conv(proj[..., OFF_GLU:N_MIX], cv_dw_w[l], cv_dw_b[l], cv_ln_g[l], cv_ln_b[l])
        merged = jnp.zeros_like(x)
        for bi, y in enumerate((y_a, y_b, y_c, y_d)):
            lo = N_MIX + bi * D_MODEL
            gate = jax.nn.sigmoid(h @ wl[:, lo:lo + D_MODEL] + b_gate[l, bi])
            merged = merged + gate * (y @ w_branch[l, bi])
        x = x + g1 * (merged @ w_out[l])
        h2 = _rmsnorm(x, norm2_g[l]) * (1.0 + sc2) + sh2
        x = x + g2 * _hier_moe(h2, router_g_w[l], router_g_b[l], router_e_w[l], router_e_b[l],
                               moe_w1[l], moe_w3[l], moe_w2[l])
    return x
```

```python
import functools
import math

import jax
import jax.numpy as jnp
from jax import lax
from jax.experimental import pallas as pl
from jax.experimental.pallas import tpu as pltpu

EPS = 1e-6
LRU_C = 8.0
LRU_CONV_TAPS = 4
POOL_WINDOWS = (2, 4, 8, 16)
POOL_HALO = 16
CONV_HALO = 32
HEAD_DIM = 64
HEAD_W = 2 * HEAD_DIM
ROPE_DIM = HEAD_DIM // 4
ROPE_HALF = ROPE_DIM // 2
ROPE_THETA = 500000.0
TOP_K = 2
NEG_BIG = -1e30
SUBLANES = 8
LANES = 128
VMEM_LIMIT_BYTES = 56 * 1024 * 1024

BF16 = jnp.bfloat16
F32 = jnp.float32


def _tile(n, pref):
    t = min(n, pref)
    assert n % t == 0, (n, pref)
    return t


def _params(sem):
    return pltpu.CompilerParams(dimension_semantics=sem, vmem_limit_bytes=VMEM_LIMIT_BYTES)


def _sigmoid(x):
    return 1.0 / (1.0 + jnp.exp(-x))


def _ada_kernel(c_ref, w_ref, b_ref, o_ref):
    c = c_ref[...]
    c_act = (c * _sigmoid(c)).astype(BF16)
    o_ref[...] = jnp.dot(c_act, w_ref[...].astype(BF16), preferred_element_type=F32) + b_ref[...]


def _ada_mod(c, ada_w, ada_b):
    depth, d, n_out = ada_w.shape
    bsz = c.shape[0]
    tn = _tile(n_out, 1024)
    return pl.pallas_call(
        _ada_kernel,
        grid=(depth, n_out // tn),
        in_specs=[
            pl.BlockSpec((bsz, d), lambda l, j: (0, 0)),
            pl.BlockSpec((None, d, tn), lambda l, j: (l, 0, j)),
            pl.BlockSpec((None, 1, tn), lambda l, j: (l, 0, j)),
        ],
        out_specs=pl.BlockSpec((None, bsz, tn), lambda l, j: (l, 0, j)),
        out_shape=jax.ShapeDtypeStruct((depth, bsz, n_out), F32),
        compiler_params=_params(("arbitrary", "arbitrary")),
    )(c, ada_w, ada_b.reshape(depth, 1, n_out))


def _modnorm(x, g, mod, shift_row, scale_row):
    y = x * lax.rsqrt(jnp.mean(x * x, axis=-1, keepdims=True) + EPS) * g
    return y * (1.0 + mod[scale_row:scale_row + 1, :]) + mod[shift_row:shift_row + 1, :]


def _norm_kernel(x_ref, g_ref, mod_ref, o_ref, *, shift_row, scale_row):
    o_ref[...] = _modnorm(x_ref[...], g_ref[...], mod_ref[...], shift_row, scale_row).astype(o_ref.dtype)


def _norm(x2, g, mod_l, seq, shift_row, scale_row):
    n, d = x2.shape
    tm = _tile(seq, 512)
    per_seq = seq // tm
    return pl.pallas_call(
        functools.partial(_norm_kernel, shift_row=shift_row, scale_row=scale_row),
        grid=(n // tm,),
        in_specs=[
            pl.BlockSpec((tm, d), lambda i: (i, 0)),
            pl.BlockSpec((1, d), lambda i: (0, 0)),
            pl.BlockSpec((None, 6, d), lambda i: (i // per_seq, 0, 0)),
        ],
        out_specs=pl.BlockSpec((tm, d), lambda i: (i, 0)),
        out_shape=jax.ShapeDtypeStruct((n, d), BF16),
        compiler_params=_params(("arbitrary",)),
    )(x2, g.reshape(1, d), mod_l)


def _matmul_kernel(a_ref, w_ref, o_ref):
    o_ref[...] = jnp.dot(a_ref[...], w_ref[...], preferred_element_type=F32).astype(o_ref.dtype)


def _matmul(a, w, n_cols):
    m, k = a.shape
    tm = _tile(m, 1024)
    tn = _tile(n_cols, 1024)
    return pl.pallas_call(
        _matmul_kernel,
        grid=(m // tm, n_cols // tn),
        in_specs=[
            pl.BlockSpec((tm, k), lambda i, j: (i, 0)),
            pl.BlockSpec((k, tn), lambda i, j: (0, j)),
        ],
        out_specs=pl.BlockSpec((tm, tn), lambda i, j: (i, j)),
        out_shape=jax.ShapeDtypeStruct((m, n_cols), BF16),
        compiler_params=_params(("arbitrary", "arbitrary")),
    )(a, w)


def _gelu_tanh(x):
    return 0.5 * x * (1.0 + jnp.tanh(math.sqrt(2.0 / math.pi) * (x + 0.044715 * (x * x * x))))


def _lru_kernel(x_ref, gate_ref, cw_ref, cb_ref, wr_ref, br_ref, wi_ref, bi_ref, lam_ref, o_ref,
                xe_ref, a_ref, u_ref, carry_ref):
    ts, db = x_ref.shape
    n_blocks, blk = wr_ref.shape[0], wr_ref.shape[1]
    halo = SUBLANES
    first = pl.program_id(1) == 0

    @pl.when(first)
    def _():
        xe_ref[0:halo, :] = jnp.zeros((halo, db), F32)
        carry_ref[...] = jnp.zeros_like(carry_ref)

    @pl.when(jnp.logical_not(first))
    def _():
        xe_ref[0:halo, :] = xe_ref[ts:ts + halo, :]

    xe_ref[halo:halo + ts, :] = x_ref[...].astype(F32)

    xa = jnp.zeros((ts, db), F32) + cb_ref[...]
    for j in range(LRU_CONV_TAPS):
        off = halo - (LRU_CONV_TAPS - 1) + j
        xa = xa + cw_ref[j:j + 1, :] * xe_ref[off:off + ts, :]

    xa16 = xa.astype(BF16)
    for hb in range(n_blocks):
        sl = slice(hb * blk, (hb + 1) * blk)
        a_ref[:, sl] = jnp.dot(xa16[:, sl], wr_ref[hb], preferred_element_type=F32)
        u_ref[:, sl] = jnp.dot(xa16[:, sl], wi_ref[hb], preferred_element_type=F32)
    r = _sigmoid(a_ref[...] + br_ref[...])
    ig = _sigmoid(u_ref[...] + bi_ref[...])
    z = -lam_ref[...]
    softplus = jnp.maximum(z, 0.0) + jnp.log(1.0 + jnp.exp(-jnp.abs(z)))
    a = jnp.exp((-LRU_C) * r * softplus)
    u = jnp.sqrt(1.0 - a * a) * (ig * xa)

    row = lax.broadcasted_iota(jnp.int32, (ts, db), 0) % SUBLANES
    for d in (1, 2, 4):
        keep = row >= d
        a_sh = jnp.where(keep, pltpu.roll(a, d, 0), 1.0)
        u_sh = jnp.where(keep, pltpu.roll(u, d, 0), 0.0)
        u = a * u_sh + u
        a = a * a_sh
    a_ref[...] = a
    u_ref[...] = u

    def body(c, h_prev):
        r0 = pl.multiple_of(c * SUBLANES, SUBLANES)
        h8 = u_ref[pl.ds(r0, SUBLANES), :] + a_ref[pl.ds(r0, SUBLANES), :] * h_prev
        u_ref[pl.ds(r0, SUBLANES), :] = h8
        return h8[SUBLANES - 1:SUBLANES, :]

    carry_ref[...] = lax.fori_loop(0, ts // SUBLANES, body, carry_ref[...])
    o_ref[...] = (_gelu_tanh(gate_ref[...].astype(F32)) * u_ref[...]).astype(o_ref.dtype)


def _lru_branch(proj, bsz, seq, col_x, col_gate, cw, cb, wr, br, wi, bi, lam):
    db = cw.shape[1]
    ts = _tile(seq, 256)
    per_seq = seq // ts
    n_blocks, blk = wr.shape[0], wr.shape[1]
    row = lambda v: v.reshape(1, db)
    vec = pl.BlockSpec((1, db), lambda b, i: (0, 0))
    wspec = pl.BlockSpec((n_blocks, blk, blk), lambda b, i: (0, 0, 0))
    return pl.pallas_call(
        _lru_kernel,
        grid=(bsz, per_seq),
        in_specs=[
            pl.BlockSpec((ts, db), lambda b, i: (b * per_seq + i, col_x)),
            pl.BlockSpec((ts, db), lambda b, i: (b * per_seq + i, col_gate)),
            pl.BlockSpec((LRU_CONV_TAPS, db), lambda b, i: (0, 0)),
            vec, wspec, vec, wspec, vec, vec,
        ],
        out_specs=pl.BlockSpec((ts, db), lambda b, i: (b * per_seq + i, 0)),
        out_shape=jax.ShapeDtypeStruct((bsz * seq, db), BF16),
        scratch_shapes=[
            pltpu.VMEM((ts + SUBLANES, db), F32),
            pltpu.VMEM((ts, db), F32),
            pltpu.VMEM((ts, db), F32),
            pltpu.VMEM((1, db), F32),
        ],
        compiler_params=_params(("arbitrary", "arbitrary")),
    )(proj, proj, cw, row(cb), wr.astype(BF16), row(br), wi.astype(BF16), row(bi), row(lam))


def _pool_kernel(x_ref, w_ref, b_ref, s_ref, o_ref, xe_ref):
    ts, db = x_ref.shape
    n_groups, pg = w_ref.shape[0], w_ref.shape[1]
    halo = POOL_HALO
    i = pl.program_id(1)

    @pl.when(i == 0)
    def _():
        xe_ref[0:halo, :] = jnp.zeros((halo, db), F32)

    @pl.when(i != 0)
    def _():
        xe_ref[0:halo, :] = xe_ref[ts:ts + halo, :]

    xe_ref[halo:halo + ts, :] = x_ref[...].astype(F32)
    t1 = (i * ts + 1 + lax.broadcasted_iota(jnp.int32, (ts, pg), 0)).astype(F32)
    for gi in range(n_groups):
        win = POOL_WINDOWS[gi]
        sl = slice(gi * pg, (gi + 1) * pg)
        x = xe_ref[halo:halo + ts, sl]
        acc = x
        for j in range(1, win):
            acc = acc + xe_ref[halo - j:halo - j + ts, sl]
        diff = acc / jnp.minimum(t1, float(win)) - x
        y = jnp.dot(diff.astype(BF16), w_ref[gi], preferred_element_type=F32) + b_ref[:, sl]
        o_ref[:, sl] = (y * s_ref[:, sl]).astype(o_ref.dtype)


def _pool_branch(proj, bsz, seq, col, w, b, scale):
    n_groups, pg = w.shape[0], w.shape[1]
    assert n_groups == len(POOL_WINDOWS)
    db = n_groups * pg
    ts = _tile(seq, 512)
    per_seq = seq // ts
    vec = pl.BlockSpec((1, db), lambda bb, i: (0, 0))
    return pl.pallas_call(
        _pool_kernel,
        grid=(bsz, per_seq),
        in_specs=[
            pl.BlockSpec((ts, db), lambda bb, i: (bb * per_seq + i, col)),
            pl.BlockSpec((n_groups, pg, pg), lambda bb, i: (0, 0, 0)),
            vec, vec,
        ],
        out_specs=pl.BlockSpec((ts, db), lambda bb, i: (bb * per_seq + i, 0)),
        out_shape=jax.ShapeDtypeStruct((bsz * seq, db), BF16),
        scratch_shapes=[pltpu.VMEM((ts + POOL_HALO, db), F32)],
        compiler_params=_params(("arbitrary", "arbitrary")),
    )(proj, w.astype(BF16), b.reshape(1, db), scale.reshape(1, db))


def _conv_kernel(val_ref, gate_ref, w_ref, b_ref, g_ref, beta_ref, o_ref, ue_ref, *, row_block):
    ts, db = val_ref.shape
    taps = w_ref.shape[0]
    halo = CONV_HALO
    i = pl.program_id(1)

    @pl.when(i == 0)
    def _():
        ue_ref[0:halo, :] = jnp.zeros((halo, db), F32)

    @pl.when(i != 0)
    def _():
        ue_ref[0:halo, :] = ue_ref[ts:ts + halo, :]

    ue_ref[halo:halo + ts, :] = val_ref[...].astype(F32) * _sigmoid(gate_ref[...].astype(F32))
    for rb in range(ts // row_block):
        r0 = rb * row_block
        acc = jnp.zeros((row_block, db), F32) + b_ref[...]
        for j in range(taps):
            off = halo - (taps - 1) + j + r0
            acc = acc + w_ref[j:j + 1, :] * ue_ref[off:off + row_block, :]
        mu = jnp.mean(acc, axis=-1, keepdims=True)
        cen = acc - mu
        var = jnp.mean(cen * cen, axis=-1, keepdims=True)
        y = cen * lax.rsqrt(var + EPS) * g_ref[...] + beta_ref[...]
        o_ref[r0:r0 + row_block, :] = (y * _sigmoid(y)).astype(o_ref.dtype)


def _conv_branch(proj, bsz, seq, col_val, col_gate, w, b, g, beta):
    taps, db = w.shape
    assert taps - 1 <= CONV_HALO
    ts = _tile(seq, 256)
    per_seq = seq // ts
    row = lambda v: v.reshape(1, db)
    vec = pl.BlockSpec((1, db), lambda bb, i: (0, 0))
    return pl.pallas_call(
        functools.partial(_conv_kernel, row_block=_tile(ts, 32)),
        grid=(bsz, per_seq),
        in_specs=[
            pl.BlockSpec((ts, db), lambda bb, i: (bb * per_seq + i, col_val)),
            pl.BlockSpec((ts, db), lambda bb, i: (bb * per_seq + i, col_gate)),
            pl.BlockSpec((taps, db), lambda bb, i: (0, 0)),
            vec, vec, vec,
        ],
        out_specs=pl.BlockSpec((ts, db), lambda bb, i: (bb * per_seq + i, 0)),
        out_shape=jax.ShapeDtypeStruct((bsz * seq, db), BF16),
        scratch_shapes=[pltpu.VMEM((ts + CONV_HALO, db), F32)],
        compiler_params=_params(("arbitrary", "arbitrary")),
    )(proj, proj, w, row(b), row(g), row(beta))


def _qk_prep_kernel(q_ref, k_ref, cos_ref, sa_ref, sb_ref, seg_ref, qg_ref, kg_ref, qo_ref, kt_ref, kn_ref):
    ts, db = q_ref.shape
    cos, sa, sb = cos_ref[...], sa_ref[...], sb_ref[...]

    def norm_rope(x, g):
        ms = jnp.dot((x * x).astype(BF16), seg_ref[...], preferred_element_type=F32)
        xn = x * lax.rsqrt(ms + EPS) * g
        return (xn * cos + pltpu.roll(xn, HEAD_W - ROPE_HALF, 1) * sa + pltpu.roll(xn, ROPE_HALF, 1) * sb)

    for h in range(db // HEAD_W):
        sl = slice(h * HEAD_W, (h + 1) * HEAD_W)
        q = norm_rope(q_ref[:, sl].astype(F32), qg_ref[...])
        qo_ref[:, sl] = (q * (1.0 / math.sqrt(HEAD_DIM))).astype(qo_ref.dtype)
        kn_ref[:, sl] = norm_rope(k_ref[:, sl].astype(F32), kg_ref[...])
    kt_ref[...] = kn_ref[...].T.astype(kt_ref.dtype)


def _qk_prep(proj, bsz, seq, col_q, col_k, db, cos_t, sa_t, sb_t, seg, qg, kg):
    ts = _tile(seq, 256)
    per_seq = seq // ts
    tab = pl.BlockSpec((ts, HEAD_W), lambda b, i: (b * per_seq + i, 0))
    vec = pl.BlockSpec((1, HEAD_W), lambda b, i: (0, 0))
    return pl.pallas_call(
        _qk_prep_kernel,
        grid=(bsz, per_seq),
        in_specs=[
            pl.BlockSpec((ts, db), lambda b, i: (b * per_seq + i, col_q)),
            pl.BlockSpec((ts, db), lambda b, i: (b * per_seq + i, col_k)),
            tab, tab, tab,
            pl.BlockSpec((HEAD_W, HEAD_W), lambda b, i: (0, 0)),
            vec, vec,
        ],
        out_specs=[
            pl.BlockSpec((ts, db), lambda b, i: (b * per_seq + i, 0)),
            pl.BlockSpec((None, db, ts), lambda b, i: (b, 0, i)),
        ],
        out_shape=[
            jax.ShapeDtypeStruct((bsz * seq, db), BF16),
            jax.ShapeDtypeStruct((bsz, db, seq), BF16),
        ],
        scratch_shapes=[pltpu.VMEM((ts, db), F32)],
        compiler_params=_params(("arbitrary", "arbitrary")),
    )(proj, proj, cos_t, sa_t, sb_t, seg, qg, kg)


def _flash_kernel(q_ref, kt_ref, v_ref, lq1_ref, lk1_ref, lq2_ref, lk2_ref, sg_ref, o_ref,
                  m_ref, l_ref, acc_ref, *, lambda_init, tk):
    tq = q_ref.shape[0]
    i = pl.program_id(2)
    lane = lax.broadcasted_iota(jnp.int32, (tq, HEAD_W), 1)
    q = q_ref[...]
    zero = jnp.zeros_like(q)
    q2 = jnp.concatenate([jnp.where(lane < HEAD_DIM, q, zero), jnp.where(lane >= HEAD_DIM, q, zero)], axis=0)
    m_ref[...] = jnp.full(m_ref.shape, NEG_BIG, F32)
    l_ref[...] = jnp.zeros(l_ref.shape, F32)
    acc_ref[...] = jnp.zeros(acc_ref.shape, F32)

    def step(j, masked):
        k0 = pl.multiple_of(j * tk, tk)
        s = jnp.dot(q2, kt_ref[:, pl.ds(k0, tk)], preferred_element_type=F32)
        if masked:
            qpos = i * tq + lax.broadcasted_iota(jnp.int32, (2 * tq, tk), 0) % tq
            kpos = k0 + lax.broadcasted_iota(jnp.int32, (2 * tq, tk), 1)
            s = jnp.where(kpos <= qpos, s, NEG_BIG)
        m_prev = m_ref[...]
        m_new = jnp.maximum(m_prev, jnp.max(s, axis=-1, keepdims=True))
        alpha = jnp.exp(m_prev - m_new)
        p = jnp.exp(s - m_new)
        l_ref[...] = alpha * l_ref[...] + jnp.sum(p, axis=-1, keepdims=True)
        acc_ref[...] = alpha * acc_ref[...] + jnp.dot(p.astype(BF16), v_ref[pl.ds(k0, tk), :],
                                                    preferred_element_type=F32)
        m_ref[...] = m_new

    n_full = (i * tq) // tk
    n_all = ((i + 1) * tq + tk - 1) // tk

    def full_body(j, c):
        step(j, False)
        return c

    def diag_body(j, c):
        step(j, True)
        return c

    lax.fori_loop(0, n_full, full_body, 0)
    lax.fori_loop(n_full, n_all, diag_body, 0)

    lam = (jnp.exp(jnp.sum(lq1_ref[...] * lk1_ref[...], axis=-1, keepdims=True))
           - jnp.exp(jnp.sum(lq2_ref[...] * lk2_ref[...], axis=-1, keepdims=True)) + lambda_init)
    o_all = acc_ref[...] / l_ref[...]
    o = o_all[0:tq, :] - lam * o_all[tq:2 * tq, :]
    o = o * lax.rsqrt(jnp.mean(o * o, axis=-1, keepdims=True) + EPS) * sg_ref[...]
    o_ref[...] = (o * (1.0 - lambda_init)).astype(o_ref.dtype)


def _flash(qn, kt, proj, bsz, seq, col_v, lq1, lk1, lq2, lk2, sub_g, lambda_init):
    db = qn.shape[1]
    heads = db // HEAD_W
    tq = _tile(seq, 256)
    tk = _tile(seq, 512)
    per_seq = seq // tq
    vec64 = pl.BlockSpec((1, HEAD_DIM), lambda b, h, i: (0, 0))
    row64 = lambda v: v.reshape(1, HEAD_DIM)
    return pl.pallas_call(
        functools.partial(_flash_kernel, lambda_init=lambda_init, tk=tk),
        grid=(bsz, heads, per_seq),
        in_specs=[
            pl.BlockSpec((tq, HEAD_W), lambda b, h, i: (b * per_seq + i, h)),
            pl.BlockSpec((None, HEAD_W, seq), lambda b, h, i: (b, h, 0)),
            pl.BlockSpec((seq, HEAD_W), lambda b, h, i: (b, col_v * heads + h)),
            vec64, vec64, vec64, vec64,
            pl.BlockSpec((1, HEAD_W), lambda b, h, i: (0, 0)),
        ],
        out_specs=pl.BlockSpec((tq, HEAD_W), lambda b, h, i: (b * per_seq + i, h)),
        out_shape=jax.ShapeDtypeStruct((bsz * seq, db), BF16),
        scratch_shapes=[
            pltpu.VMEM((2 * tq, 1), F32),
            pltpu.VMEM((2 * tq, 1), F32),
            pltpu.VMEM((2 * tq, HEAD_W), F32),
        ],
        compiler_params=_params(("arbitrary", "arbitrary", "arbitrary")),
    )(qn, kt, proj, row64(lq1), row64(lk1), row64(lq2), row64(lk2), sub_g.reshape(1, HEAD_W))


def _rope_tables(positions):
    inv_freq = ROPE_THETA ** (-jnp.arange(0, ROPE_DIM, 2, dtype=F32) / ROPE_DIM)
    ang = positions.astype(F32).reshape(-1, 1) * inv_freq
    cos, sin = jnp.cos(ang), jnp.sin(ang)
    n = ang.shape[0]
    rest = HEAD_DIM - ROPE_DIM
    zeros_h = jnp.zeros((n, ROPE_HALF), F32)
    cos_s = jnp.concatenate([cos, cos, jnp.ones((n, rest), F32)], axis=1)
    sa_s = jnp.concatenate([-sin, zeros_h, jnp.zeros((n, rest), F32)], axis=1)
    sb_s = jnp.concatenate([zeros_h, sin, jnp.zeros((n, rest), F32)], axis=1)
    rep = lambda t: jnp.concatenate([t, t], axis=1)
    return rep(cos_s), rep(sa_s), rep(sb_s)


def _merge_kernel(h_ref, ya_ref, yb_ref, yc_ref, yd_ref, wa_ref, wb_ref, wc_ref, wd_ref,
                  bg_ref, wbr_ref, o_ref):
    h = h_ref[...]
    acc = None
    for bi, (y_ref, wg_ref) in enumerate(((ya_ref, wa_ref), (yb_ref, wb_ref), (yc_ref, wc_ref), (yd_ref, wd_ref))):
        gate = _sigmoid(jnp.dot(h, wg_ref[...], preferred_element_type=F32) + bg_ref[bi:bi + 1, :])
        term = gate * jnp.dot(y_ref[...], wbr_ref[bi], preferred_element_type=F32)
        acc = term if acc is None else acc + term
    o_ref[...] = acc.astype(o_ref.dtype)


def _merge(h, ys, w_in16, n_mix, b_gate, w_branch16):
    n, d = h.shape
    db = ys[0].shape[1]
    n_branch = len(ys)
    tm = _tile(n, 512)
    tn = _tile(d, 256)
    gate_blk = [(n_mix + b * d) // tn for b in range(n_branch)]
    wspecs = [pl.BlockSpec((d, tn), functools.partial(lambda j, i, base: (0, base + j), base=gb)) for gb in gate_blk]
    yspec = pl.BlockSpec((tm, db), lambda j, i: (i, 0))
    return pl.pallas_call(
        _merge_kernel,
        grid=(d // tn, n // tm),
        in_specs=[pl.BlockSpec((tm, d), lambda j, i: (i, 0))] + [yspec] * n_branch + wspecs + [
            pl.BlockSpec((n_branch, tn), lambda j, i: (0, j)),
            pl.BlockSpec((n_branch, db, tn), lambda j, i: (0, 0, j)),
        ],
        out_specs=pl.BlockSpec((tm, tn), lambda j, i: (i, j)),
        out_shape=jax.ShapeDtypeStruct((n, d), BF16),
        compiler_params=_params(("arbitrary", "arbitrary")),
    )(h, *ys, *([w_in16] * n_branch), b_gate, w_branch16)


def _resid_matmul_kernel(a_ref, w_ref, x_ref, mod_ref, o_ref, *, gate_row):
    y = jnp.dot(a_ref[...], w_ref[...], preferred_element_type=F32)
    o_ref[...] = x_ref[...] + mod_ref[gate_row:gate_row + 1, :] * y


def _resid_matmul(a, w16, x2, mod_l, seq, gate_row):
    n, k = a.shape
    d = w16.shape[1]
    tm = _tile(seq, 1024)
    tn = _tile(d, 512)
    per_seq = seq // tm
    return pl.pallas_call(
        functools.partial(_resid_matmul_kernel, gate_row=gate_row),
        grid=(n // tm, d // tn),
        in_specs=[
            pl.BlockSpec((tm, k), lambda i, j: (i, 0)),
            pl.BlockSpec((k, tn), lambda i, j: (0, j)),
            pl.BlockSpec((tm, tn), lambda i, j: (i, j)),
            pl.BlockSpec((None, 6, tn), lambda i, j: (i // per_seq, 0, j)),
        ],
        out_specs=pl.BlockSpec((tm, tn), lambda i, j: (i, j)),
        out_shape=jax.ShapeDtypeStruct((n, d), F32),
        compiler_params=_params(("arbitrary", "arbitrary")),
    )(a, w16, x2, mod_l)


def _router_kernel(x_ref, g_ref, mod_ref, wr_ref, br_ref, h_ref, r_ref, *, n_groups, per_group):
    h2 = _modnorm(x_ref[...], g_ref[...], mod_ref[...], 3, 4)
    h_ref[...] = h2
    logits = jnp.dot(h2.astype(BF16), wr_ref[...], preferred_element_type=F32) + br_ref[...]
    lane = lax.broadcasted_iota(jnp.int32, logits.shape, 1)
    n_exp = n_groups * per_group
    is_g = lane < n_groups
    gl = jnp.where(is_g, logits, NEG_BIG)
    gmax = jnp.max(gl, axis=-1, keepdims=True)
    g_idx = jnp.min(jnp.where(gl == gmax, lane, LANES), axis=-1, keepdims=True)
    g_top = 1.0 / jnp.sum(jnp.where(is_g, jnp.exp(gl - gmax), 0.0), axis=-1, keepdims=True)
    lo = n_groups + g_idx * per_group
    in_grp = jnp.logical_and(lane >= lo, lane < lo + per_group)
    el = jnp.where(in_grp, logits, NEG_BIG)
    v1 = jnp.max(el, axis=-1, keepdims=True)
    i1 = jnp.min(jnp.where(el == v1, lane, LANES), axis=-1, keepdims=True)
    el2 = jnp.where(lane == i1, NEG_BIG, el)
    v2 = jnp.max(el2, axis=-1, keepdims=True)
    i2 = jnp.min(jnp.where(el2 == v2, lane, LANES), axis=-1, keepdims=True)
    e = jnp.exp(v2 - v1)
    w1 = g_top / (1.0 + e)
    w2 = g_top * e / (1.0 + e)
    del n_exp
    out = jnp.where(lane == 0, (i1 - n_groups).astype(F32),
                    jnp.where(lane == 1, (i2 - n_groups).astype(F32),
                              jnp.where(lane == 2, w1, jnp.where(lane == 3, w2, 0.0))))
    r_ref[...] = out


def _router(x2, g, mod_l, seq, w_router16, b_router, n_groups, per_group):
    n, d = x2.shape
    tm = _tile(seq, 256)
    per_seq = seq // tm
    return pl.pallas_call(
        functools.partial(_router_kernel, n_groups=n_groups, per_group=per_group),
        grid=(n // tm,),
        in_specs=[
            pl.BlockSpec((tm, d), lambda i: (i, 0)),
            pl.BlockSpec((1, d), lambda i: (0, 0)),
            pl.BlockSpec((None, 6, d), lambda i: (i // per_seq, 0, 0)),
            pl.BlockSpec((d, LANES), lambda i: (0, 0)),
            pl.BlockSpec((1, LANES), lambda i: (0, 0)),
        ],
        out_specs=[
            pl.BlockSpec((tm, d), lambda i: (i, 0)),
            pl.BlockSpec((tm, LANES), lambda i: (i, 0)),
        ],
        out_shape=[
            jax.ShapeDtypeStruct((n, d), F32),
            jax.ShapeDtypeStruct((n, LANES), F32),
        ],
        compiler_params=_params(("arbitrary",)),
    )(x2, g.reshape(1, d), mod_l, w_router16, b_router)


def _ffn_kernel(tile_exp_ref, n_tiles_ref, src_ref, h_hbm, w1_ref, w3_ref, w2_ref, y_ref, xbuf, sem, *, tm):
    t = pl.program_id(0)
    n_tiles = n_tiles_ref[0]
    slot = t % 2

    def row_copy(tile, r, s):
        tok = src_ref[tile * tm + r]
        return pltpu.make_async_copy(h_hbm.at[pl.ds(tok, 1), :], xbuf.at[s, pl.ds(r, 1), :], sem.at[s])

    def start_tile(tile, s):
        def body(r, c):
            row_copy(tile, r, s).start()
            return c
        lax.fori_loop(0, tm, body, 0)

    @pl.when(jnp.logical_and(t == 0, n_tiles > 0))
    def _():
        start_tile(0, 0)

    @pl.when(t + 1 < n_tiles)
    def _():
        start_tile(t + 1, 1 - slot)

    @pl.when(t < n_tiles)
    def _():
        def wait_body(r, c):
            row_copy(t, r, slot).wait()
            return c
        lax.fori_loop(0, tm, wait_body, 0)
        x = xbuf[slot].astype(BF16)
        a = jnp.dot(x, w1_ref[...], preferred_element_type=F32)
        b = jnp.dot(x, w3_ref[...], preferred_element_type=F32)
        hid = (a * _sigmoid(a) * b).astype(BF16)
        y_ref[...] = jnp.dot(hid, w2_ref[...], preferred_element_type=F32)

    @pl.when(t >= n_tiles)
    def _():
        y_ref[...] = jnp.zeros_like(y_ref)


def _ffn(h2, tile_expert, n_tiles, src_token, w1, w3, w2, tm):
    n, d = h2.shape
    n_exp, _, de = w1.shape
    max_tiles = tile_expert.shape[0]
    grid_spec = pltpu.PrefetchScalarGridSpec(
        num_scalar_prefetch=3,
        grid=(max_tiles,),
        in_specs=[
            pl.BlockSpec(memory_space=pl.ANY),
            pl.BlockSpec((None, d, de), lambda t, te, nt, src: (te[t], 0, 0)),
            pl.BlockSpec((None, d, de), lambda t, te, nt, src: (te[t], 0, 0)),
            pl.BlockSpec((None, de, d), lambda t, te, nt, src: (te[t], 0, 0)),
        ],
        out_specs=pl.BlockSpec((tm, d), lambda t, te, nt, src: (t, 0)),
        scratch_shapes=[
            pltpu.VMEM((2, tm, d), F32),
            pltpu.SemaphoreType.DMA((2,)),
        ],
    )
    return pl.pallas_call(
        functools.partial(_ffn_kernel, tm=tm),
        grid_spec=grid_spec,
        out_shape=jax.ShapeDtypeStruct((max_tiles * tm, d), F32),
        compiler_params=_params(("arbitrary",)),
    )(tile_expert, n_tiles, src_token, h2, w1, w3, w2)


def _combine_kernel(pos_ref, y_hbm, x_ref, mod_ref, w_ref, o_ref, ybuf, sem, *, tc):
    t = pl.program_id(0)
    n_steps = pl.num_programs(0)
    slot = t % 2

    def row_copy(step, r, s):
        row = pos_ref[step * (TOP_K * tc) + r]
        return pltpu.make_async_copy(y_hbm.at[pl.ds(row, 1), :], ybuf.at[s, pl.ds(r, 1), :], sem.at[s])

    def start_step(step, s):
        def body(r, c):
            row_copy(step, r, s).start()
            return c
        lax.fori_loop(0, TOP_K * tc, body, 0)

    @pl.when(t == 0)
    def _():
        start_step(0, 0)

    @pl.when(t + 1 < n_steps)
    def _():
        start_step(t + 1, 1 - slot)

    def wait_body(r, c):
        row_copy(t, r, slot).wait()
        return c
    lax.fori_loop(0, TOP_K * tc, wait_body, 0)
    w = w_ref[...]
    moe = w[:, 2:3] * ybuf[slot, 0:tc, :] + w[:, 3:4] * ybuf[slot, tc:2 * tc, :]
    o_ref[...] = x_ref[...] + mod_ref[5:6, :] * moe


def _combine(pos_steps, y_sorted, x2, mod_l, route, seq, tc):
    n, d = x2.shape
    per_seq = seq // tc
    grid_spec = pltpu.PrefetchScalarGridSpec(
        num_scalar_prefetch=1,
        grid=(n // tc,),
        in_specs=[
            pl.BlockSpec(memory_space=pl.ANY),
            pl.BlockSpec((tc, d), lambda t, pos: (t, 0)),
            pl.BlockSpec((None, 6, d), lambda t, pos: (t // per_seq, 0, 0)),
            pl.BlockSpec((tc, LANES), lambda t, pos: (t, 0)),
        ],
        out_specs=pl.BlockSpec((tc, d), lambda t, pos: (t, 0)),
        scratch_shapes=[
            pltpu.VMEM((2, TOP_K * tc, d), F32),
            pltpu.SemaphoreType.DMA((2,)),
        ],
    )
    return pl.pallas_call(
        functools.partial(_combine_kernel, tc=tc),
        grid_spec=grid_spec,
        out_shape=jax.ShapeDtypeStruct((n, d), F32),
        compiler_params=_params(("arbitrary",)),
    )(pos_steps, y_sorted, x2, mod_l, route)


def _moe_plan(route, n_exp, tm, tc):
    n = route.shape[0]
    eid = route[:, 0:TOP_K].astype(jnp.int32)
    flat_e = eid.reshape(-1)
    onehot = (flat_e[:, None] == jnp.arange(n_exp, dtype=jnp.int32)[None, :]).astype(jnp.int32)
    csum = jnp.cumsum(onehot, axis=0)
    rank = jnp.sum(csum * onehot, axis=1) - 1
    counts = csum[-1]
    tiles = (counts + tm - 1) // tm
    tile_end = jnp.cumsum(tiles)
    tile_start = tile_end - tiles
    n_tiles = tile_end[-1]
    pos = (tile_start * tm)[flat_e] + rank
    max_tiles = (TOP_K * n) // tm + n_exp
    token = jnp.arange(TOP_K * n, dtype=jnp.int32) // TOP_K
    src_token = jnp.zeros((max_tiles * tm,), jnp.int32).at[pos].set(token)
    tix = jnp.minimum(jnp.arange(max_tiles, dtype=jnp.int32), n_tiles - 1)
    tile_expert = jnp.searchsorted(tile_end, tix, side="right").astype(jnp.int32)
    pos_steps = pos.reshape(n // tc, tc, TOP_K).transpose(0, 2, 1).reshape(-1)
    return tile_expert, n_tiles.reshape(1).astype(jnp.int32), src_token, pos_steps.astype(jnp.int32)


def kernel(x, c, positions, ada_w, ada_b, norm1_g, norm2_g, w_in, b_gate, lru_conv_w, lru_conv_b, lru_wr, lru_br, lru_wi, lru_bi, lru_lambda, pool_w, pool_b, pool_scale, q_norm_g, k_norm_g, lam_q1, lam_k1, lam_q2, lam_k2, subln_g, cv_dw_w, cv_dw_b, cv_ln_g, cv_ln_b, w_branch, w_out, router_g_w, router_g_b, router_e_w, router_e_b, moe_w1, moe_w3, moe_w2):
    bsz, seq, d = x.shape
    depth = ada_w.shape[0]
    db = lru_conv_w.shape[2]
    n_branch = w_branch.shape[1]
    n_mix = w_in.shape[2] - n_branch * d
    assert n_mix == 8 * db and q_norm_g.shape[1] == HEAD_DIM
    n_groups = router_g_w.shape[2]
    n_exp = router_e_w.shape[2]
    per_group = n_exp // n_groups
    assert n_groups + n_exp <= LANES
    n = bsz * seq
    col = {name: idx for idx, name in enumerate(("lru_x", "lru_gate", "pool", "q", "k", "v", "glu_val", "glu_gate"))}

    mod = _ada_mod(c, ada_w, ada_b).reshape(depth, bsz, 6, d)
    cos_t, sa_t, sb_t = _rope_tables(positions)
    seg = jnp.kron(jnp.eye(HEAD_W // HEAD_DIM, dtype=F32), jnp.full((HEAD_DIM, HEAD_DIM), 1.0 / HEAD_DIM, F32)).astype(BF16)
    dup = lambda g: jnp.concatenate([g, g]).reshape(1, HEAD_W)
    ffn_tm = _tile(TOP_K * n, 256)
    comb_tc = _tile(seq, 128)

    x2 = x.reshape(n, d)
    for l in range(depth):
        mod_l = mod[l]
        w_in16 = w_in[l].astype(BF16)
        h = _norm(x2, norm1_g[l], mod_l, seq, 0, 1)
        proj = _matmul(h, w_in16, n_mix)
        y_a = _lru_branch(proj, bsz, seq, col["lru_x"], col["lru_gate"], lru_conv_w[l], lru_conv_b[l],
                          lru_wr[l], lru_br[l], lru_wi[l], lru_bi[l], lru_lambda[l])
        y_b = _pool_branch(proj, bsz, seq, col["pool"], pool_w[l], pool_b[l], pool_scale[l])
        qn, kt = _qk_prep(proj, bsz, seq, col["q"], col["k"], db, cos_t, sa_t, sb_t, seg,
                          dup(q_norm_g[l]), dup(k_norm_g[l]))
        lambda_init = 0.8 - 0.6 * math.exp(-0.3 * l)
        y_c = _flash(qn, kt, proj, bsz, seq, col["v"], lam_q1[l], lam_k1[l], lam_q2[l], lam_k2[l],
                     subln_g[l], lambda_init)
        y_d = _conv_branch(proj, bsz, seq, col["glu_val"], col["glu_gate"], cv_dw_w[l], cv_dw_b[l],
                           cv_ln_g[l], cv_ln_b[l])
        merged = _merge(h, (y_a, y_b, y_c, y_d), w_in16, n_mix, b_gate[l], w_branch[l].astype(BF16))
        x2 = _resid_matmul(merged, w_out[l].astype(BF16), x2, mod_l, seq, 2)

        w_router = jnp.concatenate([router_g_w[l], router_e_w[l]], axis=1)
        w_router = jnp.pad(w_router, ((0, 0), (0, LANES - w_router.shape[1]))).astype(BF16)
        b_router = jnp.pad(jnp.concatenate([router_g_b[l], router_e_b[l]]), (0, LANES - n_groups - n_exp)).reshape(1, LANES)
        h2, route = _router(x2, norm2_g[l], mod_l, seq, w_router, b_router, n_groups, per_group)
        tile_expert, n_tiles, src_token, pos_steps = _moe_plan(route, n_exp, ffn_tm, comb_tc)
        y_sorted = _ffn(h2, tile_expert, n_tiles, src_token, moe_w1[l].astype(BF16), moe_w3[l].astype(BF16),
                        moe_w2[l].astype(BF16), ffn_tm)
        x2 = _combine(pos_steps, y_sorted, x2, mod_l, route, seq, comb_tc)
    return x2.reshape(bsz, seq, d)
```

```python
import functools
import math

import jax
import jax.numpy as jnp
from jax import lax
from jax.experimental import pallas as pl
from jax.experimental.pallas import tpu as pltpu

EPS = 1e-6
LRU_C = 8.0
LRU_CONV_TAPS = 4
POOL_WINDOWS = (2, 4, 8, 16)
POOL_HALO = 16
CONV_HALO = 32
HEAD_DIM = 64
HEAD_W = 2 * HEAD_DIM
ROPE_DIM = HEAD_DIM // 4
ROPE_HALF = ROPE_DIM // 2
ROPE_THETA = 500000.0
TOP_K = 2
NEG_BIG = -1e30
LOG2_E = math.log2(math.e)
SUBLANES = 8
LANES = 128
VMEM_LIMIT_BYTES = 56 * 1024 * 1024

BF16 = jnp.bfloat16
F32 = jnp.float32


def _tile(n, pref):
    t = min(n, pref)
    assert n % t == 0, (n, pref)
    return t


def _params(sem, **kw):
    return pltpu.CompilerParams(dimension_semantics=sem, vmem_limit_bytes=VMEM_LIMIT_BYTES, **kw)


def _sigmoid(x):
    return 1.0 / (1.0 + jnp.exp(-x))


def _ada_kernel(c_ref, w_ref, b_ref, o_ref):
    c = c_ref[...]
    c_act = (c * _sigmoid(c)).astype(BF16)
    o_ref[...] = jnp.dot(c_act, w_ref[...].astype(BF16), preferred_element_type=F32) + b_ref[...]


def _ada_mod(c, ada_w, ada_b):
    depth, d, n_out = ada_w.shape
    bsz = c.shape[0]
    tn = _tile(n_out, 1024)
    return pl.pallas_call(
        _ada_kernel,
        name="ada_mod",
        grid=(depth, n_out // tn),
        in_specs=[
            pl.BlockSpec((bsz, d), lambda l, j: (0, 0)),
            pl.BlockSpec((None, d, tn), lambda l, j: (l, 0, j)),
            pl.BlockSpec((None, 1, tn), lambda l, j: (l, 0, j)),
        ],
        out_specs=pl.BlockSpec((None, bsz, tn), lambda l, j: (l, 0, j)),
        out_shape=jax.ShapeDtypeStruct((depth, bsz, n_out), F32),
        compiler_params=_params(("arbitrary", "arbitrary")),
    )(c, ada_w, ada_b.reshape(depth, 1, n_out))


def _modnorm(x, g, mod, shift_row, scale_row):
    y = x * lax.rsqrt(jnp.mean(x * x, axis=-1, keepdims=True) + EPS) * g
    return y * (1.0 + mod[scale_row:scale_row + 1, :]) + mod[shift_row:shift_row + 1, :]


def _norm_kernel(x_ref, g_ref, mod_ref, o_ref, *, shift_row, scale_row):
    o_ref[...] = _modnorm(x_ref[...], g_ref[...], mod_ref[...], shift_row, scale_row).astype(o_ref.dtype)


def _norm(x2, g, mod_l, seq, shift_row, scale_row):
    n, d = x2.shape
    tm = _tile(seq, 512)
    per_seq = seq // tm
    return pl.pallas_call(
        functools.partial(_norm_kernel, shift_row=shift_row, scale_row=scale_row),
        name="mod_norm",
        grid=(n // tm,),
        in_specs=[
            pl.BlockSpec((tm, d), lambda i: (i, 0)),
            pl.BlockSpec((1, d), lambda i: (0, 0)),
            pl.BlockSpec((None, 6, d), lambda i: (i // per_seq, 0, 0)),
        ],
        out_specs=pl.BlockSpec((tm, d), lambda i: (i, 0)),
        out_shape=jax.ShapeDtypeStruct((n, d), BF16),
        compiler_params=_params(("arbitrary",)),
    )(x2, g.reshape(1, d), mod_l)


def _matmul_kernel(a_ref, w_ref, o_ref):
    o_ref[...] = jnp.dot(a_ref[...], w_ref[...], preferred_element_type=F32).astype(o_ref.dtype)


def _matmul(a, w, layer, n_cols):
    m, k = a.shape
    tm = _tile(m, 1024)
    tn = _tile(n_cols, 1024)
    return pl.pallas_call(
        _matmul_kernel,
        name="in_proj",
        grid=(m // tm, n_cols // tn),
        in_specs=[
            pl.BlockSpec((tm, k), lambda i, j: (i, 0)),
            pl.BlockSpec((None, k, tn), lambda i, j: (layer, 0, j)),
        ],
        out_specs=pl.BlockSpec((tm, tn), lambda i, j: (i, j)),
        out_shape=jax.ShapeDtypeStruct((m, n_cols), BF16),
        compiler_params=_params(("arbitrary", "arbitrary")),
    )(a, w)


def _gelu_tanh(x):
    return 0.5 * x * (1.0 + jnp.tanh(math.sqrt(2.0 / math.pi) * (x + 0.044715 * (x * x * x))))


def _lru_kernel(x_ref, gate_ref, cw_ref, cb_ref, wr_ref, br_ref, wi_ref, bi_ref, lam_ref, o_ref,
                xe_ref, a_ref, u_ref, carry_ref):
    ts, db = x_ref.shape
    n_blocks, blk = wr_ref.shape[0], wr_ref.shape[1]
    halo = SUBLANES
    first = pl.program_id(1) == 0

    @pl.when(first)
    def _():
        xe_ref[0:halo, :] = jnp.zeros((halo, db), F32)
        carry_ref[...] = jnp.zeros_like(carry_ref)

    @pl.when(jnp.logical_not(first))
    def _():
        xe_ref[0:halo, :] = xe_ref[ts:ts + halo, :]

    xe_ref[halo:halo + ts, :] = x_ref[...].astype(F32)

    xa = jnp.zeros((ts, db), F32) + cb_ref[...]
    for j in range(LRU_CONV_TAPS):
        off = halo - (LRU_CONV_TAPS - 1) + j
        xa = xa + cw_ref[j:j + 1, :] * xe_ref[off:off + ts, :]

    xa16 = xa.astype(BF16)
    for hb in range(n_blocks):
        sl = slice(hb * blk, (hb + 1) * blk)
        a_ref[:, sl] = jnp.dot(xa16[:, sl], wr_ref[hb], preferred_element_type=F32)
        u_ref[:, sl] = jnp.dot(xa16[:, sl], wi_ref[hb], preferred_element_type=F32)
    r = _sigmoid(a_ref[...] + br_ref[...])
    ig = _sigmoid(u_ref[...] + bi_ref[...])
    z = -lam_ref[...]
    softplus = jnp.maximum(z, 0.0) + jnp.log(1.0 + jnp.exp(-jnp.abs(z)))
    a = jnp.exp((-LRU_C) * r * softplus)
    u = jnp.sqrt(1.0 - a * a) * (ig * xa)

    row = lax.broadcasted_iota(jnp.int32, (ts, db), 0) % SUBLANES
    for d in (1, 2, 4):
        keep = row >= d
        a_sh = jnp.where(keep, pltpu.roll(a, d, 0), 1.0)
        u_sh = jnp.where(keep, pltpu.roll(u, d, 0), 0.0)
        u = a * u_sh + u
        a = a * a_sh
    a_ref[...] = a
    u_ref[...] = u

    def body(c, h_prev):
        r0 = pl.multiple_of(c * SUBLANES, SUBLANES)
        h8 = u_ref[pl.ds(r0, SUBLANES), :] + a_ref[pl.ds(r0, SUBLANES), :] * h_prev
        u_ref[pl.ds(r0, SUBLANES), :] = h8
        return h8[SUBLANES - 1:SUBLANES, :]

    carry_ref[...] = lax.fori_loop(0, ts // SUBLANES, body, carry_ref[...])
    o_ref[...] = (_gelu_tanh(gate_ref[...].astype(F32)) * u_ref[...]).astype(o_ref.dtype)


def _lru_branch(proj, bsz, seq, col_x, col_gate, cw, cb, wr, br, wi, bi, lam):
    db = cw.shape[1]
    ts = _tile(seq, 256)
    per_seq = seq // ts
    n_blocks, blk = wr.shape[0], wr.shape[1]
    row = lambda v: v.reshape(1, db)
    vec = pl.BlockSpec((1, db), lambda b, i: (0, 0))
    wspec = pl.BlockSpec((n_blocks, blk, blk), lambda b, i: (0, 0, 0))
    return pl.pallas_call(
        _lru_kernel,
        name="lru_branch",
        grid=(bsz, per_seq),
        in_specs=[
            pl.BlockSpec((ts, db), lambda b, i: (b * per_seq + i, col_x)),
            pl.BlockSpec((ts, db), lambda b, i: (b * per_seq + i, col_gate)),
            pl.BlockSpec((LRU_CONV_TAPS, db), lambda b, i: (0, 0)),
            vec, wspec, vec, wspec, vec, vec,
        ],
        out_specs=pl.BlockSpec((ts, db), lambda b, i: (b * per_seq + i, 0)),
        out_shape=jax.ShapeDtypeStruct((bsz * seq, db), BF16),
        scratch_shapes=[
            pltpu.VMEM((ts + SUBLANES, db), F32),
            pltpu.VMEM((ts, db), F32),
            pltpu.VMEM((ts, db), F32),
            pltpu.VMEM((1, db), F32),
        ],
        compiler_params=_params(("arbitrary", "arbitrary")),
    )(proj, proj, cw, row(cb), wr.astype(BF16), row(br), wi.astype(BF16), row(bi), row(lam))


def _pool_kernel(x_ref, w_ref, b_ref, s_ref, o_ref, xe_ref):
    ts, db = x_ref.shape
    n_groups, pg = w_ref.shape[0], w_ref.shape[1]
    halo = POOL_HALO
    i = pl.program_id(1)

    @pl.when(i == 0)
    def _():
        xe_ref[0:halo, :] = jnp.zeros((halo, db), F32)

    @pl.when(i != 0)
    def _():
        xe_ref[0:halo, :] = xe_ref[ts:ts + halo, :]

    xe_ref[halo:halo + ts, :] = x_ref[...].astype(F32)
    t1 = (i * ts + 1 + lax.broadcasted_iota(jnp.int32, (ts, pg), 0)).astype(F32)
    for gi in range(n_groups):
        win = POOL_WINDOWS[gi]
        sl = slice(gi * pg, (gi + 1) * pg)
        x = xe_ref[halo:halo + ts, sl]
        acc = x
        for j in range(1, win):
            acc = acc + xe_ref[halo - j:halo - j + ts, sl]
        diff = acc / jnp.minimum(t1, float(win)) - x
        y = jnp.dot(diff.astype(BF16), w_ref[gi], preferred_element_type=F32) + b_ref[:, sl]
        o_ref[:, sl] = (y * s_ref[:, sl]).astype(o_ref.dtype)


def _pool_branch(proj, bsz, seq, col, w, b, scale):
    n_groups, pg = w.shape[0], w.shape[1]
    assert n_groups == len(POOL_WINDOWS)
    db = n_groups * pg
    ts = _tile(seq, 512)
    per_seq = seq // ts
    vec = pl.BlockSpec((1, db), lambda bb, i: (0, 0))
    return pl.pallas_call(
        _pool_kernel,
        name="pool_branch",
        grid=(bsz, per_seq),
        in_specs=[
            pl.BlockSpec((ts, db), lambda bb, i: (bb * per_seq + i, col)),
            pl.BlockSpec((n_groups, pg, pg), lambda bb, i: (0, 0, 0)),
            vec, vec,
        ],
        out_specs=pl.BlockSpec((ts, db), lambda bb, i: (bb * per_seq + i, 0)),
        out_shape=jax.ShapeDtypeStruct((bsz * seq, db), BF16),
        scratch_shapes=[pltpu.VMEM((ts + POOL_HALO, db), F32)],
        compiler_params=_params(("arbitrary", "arbitrary")),
    )(proj, w.astype(BF16), b.reshape(1, db), scale.reshape(1, db))


def _conv_kernel(val_ref, gate_ref, w_ref, b_ref, g_ref, beta_ref, o_ref, ue_ref, sh_ref, *, row_block):
    ts, db = val_ref.shape
    taps = w_ref.shape[0]
    halo = CONV_HALO
    i = pl.program_id(1)

    @pl.when(i == 0)
    def _():
        ue_ref[0:halo, :] = jnp.zeros((halo, db), F32)

    @pl.when(i != 0)
    def _():
        ue_ref[0:halo, :] = ue_ref[ts:ts + halo, :]

    ue_ref[halo:halo + ts, :] = val_ref[...].astype(F32) * _sigmoid(gate_ref[...].astype(F32))
    span = sh_ref.shape[1]
    for p in range(1, SUBLANES):
        sh_ref[p - 1] = ue_ref[p:p + span, :]
    base = halo - (taps - 1)
    for rb in range(ts // row_block):
        r0 = rb * row_block
        acc = jnp.zeros((row_block, db), F32) + b_ref[...]
        for j in range(taps):
            a, p = divmod(base + j, SUBLANES)
            row = r0 + a * SUBLANES
            window = ue_ref[row:row + row_block, :] if p == 0 else sh_ref[p - 1, row:row + row_block, :]
            acc = acc + w_ref[j:j + 1, :] * window
        mu = jnp.mean(acc, axis=-1, keepdims=True)
        cen = acc - mu
        var = jnp.mean(cen * cen, axis=-1, keepdims=True)
        y = cen * lax.rsqrt(var + EPS) * g_ref[...] + beta_ref[...]
        o_ref[r0:r0 + row_block, :] = (y * _sigmoid(y)).astype(o_ref.dtype)


def _conv_branch(proj, bsz, seq, col_val, col_gate, w, b, g, beta):
    taps, db = w.shape
    assert taps - 1 <= CONV_HALO
    ts = _tile(seq, 256)
    per_seq = seq // ts
    row = lambda v: v.reshape(1, db)
    vec = pl.BlockSpec((1, db), lambda bb, i: (0, 0))
    return pl.pallas_call(
        functools.partial(_conv_kernel, row_block=_tile(ts, 32)),
        name="conv_branch",
        grid=(bsz, per_seq),
        in_specs=[
            pl.BlockSpec((ts, db), lambda bb, i: (bb * per_seq + i, col_val)),
            pl.BlockSpec((ts, db), lambda bb, i: (bb * per_seq + i, col_gate)),
            pl.BlockSpec((taps, db), lambda bb, i: (0, 0)),
            vec, vec, vec,
        ],
        out_specs=pl.BlockSpec((ts, db), lambda bb, i: (bb * per_seq + i, 0)),
        out_shape=jax.ShapeDtypeStruct((bsz * seq, db), BF16),
        scratch_shapes=[
            pltpu.VMEM((ts + CONV_HALO, db), F32),
            pltpu.VMEM((SUBLANES - 1, ts + CONV_HALO - SUBLANES, db), F32),
        ],
        compiler_params=_params(("arbitrary", "arbitrary")),
    )(proj, proj, w, row(b), row(g), row(beta))


def _qk_prep_kernel(q_ref, k_ref, cos_ref, sa_ref, sb_ref, seg_ref, qg_ref, kg_ref, qo_ref, kt_ref, kn_ref):
    ts, db = q_ref.shape
    cos, sa, sb = cos_ref[...], sa_ref[...], sb_ref[...]

    def norm_rope(x, g):
        ms = jnp.dot((x * x).astype(BF16), seg_ref[...], preferred_element_type=F32)
        xn = x * lax.rsqrt(ms + EPS) * g
        return (xn * cos + pltpu.roll(xn, HEAD_W - ROPE_HALF, 1) * sa + pltpu.roll(xn, ROPE_HALF, 1) * sb)

    for h in range(db // HEAD_W):
        sl = slice(h * HEAD_W, (h + 1) * HEAD_W)
        q = norm_rope(q_ref[:, sl].astype(F32), qg_ref[...])
        qo_ref[:, sl] = (q * (LOG2_E / math.sqrt(HEAD_DIM))).astype(qo_ref.dtype)
        kn_ref[:, sl] = norm_rope(k_ref[:, sl].astype(F32), kg_ref[...])
    kt_ref[...] = kn_ref[...].T.astype(kt_ref.dtype)


def _qk_prep(proj, bsz, seq, col_q, col_k, db, cos_t, sa_t, sb_t, seg, qg, kg):
    ts = _tile(seq, 256)
    per_seq = seq // ts
    tab = pl.BlockSpec((ts, HEAD_W), lambda b, i: (b * per_seq + i, 0))
    vec = pl.BlockSpec((1, HEAD_W), lambda b, i: (0, 0))
    return pl.pallas_call(
        _qk_prep_kernel,
        name="qk_prep",
        grid=(bsz, per_seq),
        in_specs=[
            pl.BlockSpec((ts, db), lambda b, i: (b * per_seq + i, col_q)),
            pl.BlockSpec((ts, db), lambda b, i: (b * per_seq + i, col_k)),
            tab, tab, tab,
            pl.BlockSpec((HEAD_W, HEAD_W), lambda b, i: (0, 0)),
            vec, vec,
        ],
        out_specs=[
            pl.BlockSpec((ts, db), lambda b, i: (b * per_seq + i, 0)),
            pl.BlockSpec((None, db, ts), lambda b, i: (b, 0, i)),
        ],
        out_shape=[
            jax.ShapeDtypeStruct((bsz * seq, db), BF16),
            jax.ShapeDtypeStruct((bsz, db, seq), BF16),
        ],
        scratch_shapes=[pltpu.VMEM((ts, db), F32)],
        compiler_params=_params(("arbitrary", "arbitrary")),
    )(proj, proj, cos_t, sa_t, sb_t, seg, qg, kg)


def _flash_kernel(q_ref, kt_ref, v_ref, lq1_ref, lk1_ref, lq2_ref, lk2_ref, sg_ref, o_ref,
                  m_ref, l_ref, acc_ref, p_ref, *, lambda_init, tk):
    tq = q_ref.shape[0]
    rows = 2 * tq
    i = pl.program_id(2)
    lane = lax.broadcasted_iota(jnp.int32, (tq, HEAD_W), 1)
    q = q_ref[...]
    zero = jnp.zeros_like(q)
    q2 = jnp.concatenate([jnp.where(lane < HEAD_DIM, q, zero), jnp.where(lane >= HEAD_DIM, q, zero)], axis=0)
    m_ref[...] = jnp.full(m_ref.shape, NEG_BIG, F32)
    l_ref[...] = jnp.zeros(l_ref.shape, F32)
    acc_ref[...] = jnp.zeros(acc_ref.shape, F32)
    p_ref[...] = jnp.zeros(p_ref.shape, p_ref.dtype)

    def pv(j):
        k0 = pl.multiple_of(j * tk, tk)
        return jnp.dot(p_ref[...], v_ref[pl.ds(k0, tk), :], preferred_element_type=F32)

    def step(j, masked):
        k0 = pl.multiple_of(j * tk, tk)
        s = jnp.dot(q2, kt_ref[:, pl.ds(k0, tk)], preferred_element_type=F32)
        pv_prev = pv(jnp.maximum(j - 1, 0))
        if masked:
            qpos = i * tq + lax.broadcasted_iota(jnp.int32, (rows, tk), 0) % tq
            kpos = k0 + lax.broadcasted_iota(jnp.int32, (rows, tk), 1)
            s = jnp.where(kpos <= qpos, s, NEG_BIG)
        cols = [s[:, cc * LANES:(cc + 1) * LANES] for cc in range(tk // LANES)]
        m_prev = m_ref[...]
        m_next = jnp.maximum(m_prev, jnp.max(functools.reduce(jnp.maximum, cols), axis=-1, keepdims=True))
        alpha = jnp.exp2(m_prev - m_next)
        ps = [jnp.exp2(col - m_next) for col in cols]
        l_ref[...] = alpha * l_ref[...] + functools.reduce(jnp.add, ps)
        acc_ref[...] = alpha * (acc_ref[...] + pv_prev)
        p_ref[...] = jnp.concatenate([x.astype(p_ref.dtype) for x in ps], axis=1)
        m_ref[...] = m_next

    n_full = (i * tq) // tk
    n_all = ((i + 1) * tq + tk - 1) // tk

    def full_body(j, c):
        step(j, False)
        return c

    def diag_body(j, c):
        step(j, True)
        return c

    lax.fori_loop(0, n_full, full_body, 0)
    lax.fori_loop(n_full, n_all, diag_body, 0)

    lam = (jnp.exp(jnp.sum(lq1_ref[...] * lk1_ref[...], axis=-1, keepdims=True))
           - jnp.exp(jnp.sum(lq2_ref[...] * lk2_ref[...], axis=-1, keepdims=True)) + lambda_init)
    o_all = (acc_ref[...] + pv(n_all - 1)) / jnp.sum(l_ref[...], axis=-1, keepdims=True)
    o = o_all[0:tq, :] - lam * o_all[tq:rows, :]
    o = o * lax.rsqrt(jnp.mean(o * o, axis=-1, keepdims=True) + EPS) * sg_ref[...]
    o_ref[...] = (o * (1.0 - lambda_init)).astype(o_ref.dtype)


def _flash(qn, kt, proj, bsz, seq, col_v, lq1, lk1, lq2, lk2, sub_g, lambda_init):
    db = qn.shape[1]
    heads = db // HEAD_W
    tq = _tile(seq, 512)
    tk = _tile(seq, 512)
    per_seq = seq // tq
    vec64 =pl.BlockSpec((1, HEAD_DIM), lambda b, h, i: (0, 0))
    row64 = lambda v: v.reshape(1, HEAD_DIM)
    return pl.pallas_call(
        functools.partial(_flash_kernel, lambda_init=lambda_init, tk=tk),
        name="diff_attn",
        grid=(bsz, heads, per_seq),
        in_specs=[
            pl.BlockSpec((tq, HEAD_W), lambda b, h, i: (b * per_seq + i, h)),
            pl.BlockSpec((None, HEAD_W, seq), lambda b, h, i: (b, h, 0)),
            pl.BlockSpec((seq, HEAD_W), lambda b, h, i: (b, col_v * heads + h)),
            vec64, vec64, vec64, vec64,
            pl.BlockSpec((1, HEAD_W), lambda b, h, i: (0, 0)),
        ],
        out_specs=pl.BlockSpec((tq, HEAD_W), lambda b, h, i: (b * per_seq + i, h)),
        out_shape=jax.ShapeDtypeStruct((bsz * seq, db), BF16),
        scratch_shapes=[pltpu.VMEM((2 * tq, HEAD_W), F32)] * 3 + [pltpu.VMEM((2 * tq, tk), BF16)],
        compiler_params=_params(("arbitrary", "arbitrary", "arbitrary")),
    )(qn, kt, proj, row64(lq1), row64(lk1), row64(lq2), row64(lk2), sub_g.reshape(1, HEAD_W))


def _rope_tables(positions):
    inv_freq = ROPE_THETA ** (-jnp.arange(0, ROPE_DIM, 2, dtype=F32) / ROPE_DIM)
    ang = positions.astype(F32).reshape(-1, 1) * inv_freq
    cos, sin = jnp.cos(ang), jnp.sin(ang)
    n = ang.shape[0]
    rest = HEAD_DIM - ROPE_DIM
    zeros_h = jnp.zeros((n, ROPE_HALF), F32)
    cos_s = jnp.concatenate([cos, cos, jnp.ones((n, rest), F32)], axis=1)
    sa_s = jnp.concatenate([-sin, zeros_h, jnp.zeros((n, rest), F32)], axis=1)
    sb_s = jnp.concatenate([zeros_h, sin, jnp.zeros((n, rest), F32)], axis=1)
    rep = lambda t: jnp.concatenate([t, t], axis=1)
    return rep(cos_s), rep(sa_s), rep(sb_s)


def _merge_kernel(h_ref, ya_ref, yb_ref, yc_ref, yd_ref, wa_ref, wb_ref, wc_ref, wd_ref,
                  bg_ref, wbr_ref, o_ref):
    h = h_ref[...]
    acc = None
    for bi, (y_ref, wg_ref) in enumerate(((ya_ref, wa_ref), (yb_ref, wb_ref), (yc_ref, wc_ref), (yd_ref, wd_ref))):
        gate = _sigmoid(jnp.dot(h, wg_ref[...], preferred_element_type=F32) + bg_ref[bi:bi + 1, :])
        term = gate * jnp.dot(y_ref[...], wbr_ref[bi], preferred_element_type=F32)
        acc = term if acc is None else acc + term
    o_ref[...] = acc.astype(o_ref.dtype)


def _merge(h, ys, w_in16, layer, n_mix, b_gate, w_branch16):
    n, d = h.shape
    db = ys[0].shape[1]
    n_branch = len(ys)
    tm = _tile(n, 512)
    tn = _tile(d, 256)
    gate_blk = [(n_mix + b * d) // tn for b in range(n_branch)]
    wspecs = [pl.BlockSpec((None, d, tn), functools.partial(lambda j, i, base: (layer, 0, base + j), base=gb))
              for gb in gate_blk]
    yspec = pl.BlockSpec((tm, db), lambda j, i: (i, 0))
    return pl.pallas_call(
        _merge_kernel,
        name="gated_merge",
        grid=(d // tn, n // tm),
        in_specs=[pl.BlockSpec((tm, d), lambda j, i: (i, 0))] + [yspec] * n_branch + wspecs + [
            pl.BlockSpec((n_branch, tn), lambda j, i: (0, j)),
            pl.BlockSpec((None, n_branch, db, tn), lambda j, i: (layer, 0, 0, j)),
        ],
        out_specs=pl.BlockSpec((tm, tn), lambda j, i: (i, j)),
        out_shape=jax.ShapeDtypeStruct((n, d), BF16),
        compiler_params=_params(("arbitrary", "arbitrary")),
    )(h, *ys, *([w_in16] * n_branch), b_gate, w_branch16)


def _resid_matmul_kernel(a_ref, w_ref, x_ref, mod_ref, o_ref, *, gate_row):
    y = jnp.dot(a_ref[...], w_ref[...], preferred_element_type=F32)
    o_ref[...] = x_ref[...] + mod_ref[gate_row:gate_row + 1, :] * y


def _resid_matmul(a, w16, layer, x2, mod_l, seq, gate_row):
    n, k = a.shape
    d = w16.shape[2]
    tm = _tile(seq, 1024)
    tn = _tile(d, 512)
    per_seq = seq // tm
    return pl.pallas_call(
        functools.partial(_resid_matmul_kernel, gate_row=gate_row),
        name="out_proj_resid",
        grid=(n // tm, d // tn),
        in_specs=[
            pl.BlockSpec((tm, k), lambda i, j: (i, 0)),
            pl.BlockSpec((None, k, tn), lambda i, j: (layer, 0, j)),
            pl.BlockSpec((tm, tn), lambda i, j: (i, j)),
            pl.BlockSpec((None, 6, tn), lambda i, j: (i // per_seq, 0, j)),
        ],
        out_specs=pl.BlockSpec((tm, tn), lambda i, j: (i, j)),
        out_shape=jax.ShapeDtypeStruct((n, d), F32),
        compiler_params=_params(("arbitrary", "arbitrary")),
    )(a, w16, x2, mod_l)


def _router_kernel(x_ref, g_ref, mod_ref, wr_ref, br_ref, h_ref, r_ref, *, n_groups, per_group):
    h2 = _modnorm(x_ref[...], g_ref[...], mod_ref[...], 3, 4)
    h_ref[...] = h2
    logits = jnp.dot(h2.astype(BF16), wr_ref[...], preferred_element_type=F32) + br_ref[...]
    lane = lax.broadcasted_iota(jnp.int32, logits.shape, 1)
    n_exp = n_groups * per_group
    is_g = lane < n_groups
    gl = jnp.where(is_g, logits, NEG_BIG)
    gmax = jnp.max(gl, axis=-1, keepdims=True)
    g_idx = jnp.min(jnp.where(gl == gmax, lane, LANES), axis=-1, keepdims=True)
    g_top = 1.0 / jnp.sum(jnp.where(is_g, jnp.exp(gl - gmax), 0.0), axis=-1, keepdims=True)
    lo = n_groups + g_idx * per_group
    in_grp = jnp.logical_and(lane >= lo, lane < lo + per_group)
    el = jnp.where(in_grp, logits, NEG_BIG)
    v1 = jnp.max(el, axis=-1, keepdims=True)
    i1 = jnp.min(jnp.where(el == v1, lane, LANES), axis=-1, keepdims=True)
    el2 = jnp.where(lane == i1, NEG_BIG, el)
    v2 = jnp.max(el2, axis=-1, keepdims=True)
    i2 = jnp.min(jnp.where(el2 == v2, lane, LANES), axis=-1, keepdims=True)
    e = jnp.exp(v2 - v1)
    w1 = g_top / (1.0 + e)
    w2 = g_top * e / (1.0 + e)
    del n_exp
    out = jnp.where(lane == 0, (i1 - n_groups).astype(F32),
                    jnp.where(lane == 1, (i2 - n_groups).astype(F32),
                              jnp.where(lane == 2, w1, jnp.where(lane == 3, w2, 0.0))))
    r_ref[...] = out


def _router(x2, g, mod_l, seq, w_router16, b_router, n_groups, per_group):
    n, d = x2.shape
    tm = _tile(seq, 256)
    per_seq = seq // tm
    return pl.pallas_call(
        functools.partial(_router_kernel, n_groups=n_groups, per_group=per_group),
        name="moe_router",
        grid=(n // tm,),
        in_specs=[
            pl.BlockSpec((tm, d), lambda i: (i, 0)),
            pl.BlockSpec((1, d), lambda i: (0, 0)),
            pl.BlockSpec((None, 6, d), lambda i: (i // per_seq, 0, 0)),
            pl.BlockSpec((d, LANES), lambda i: (0, 0)),
            pl.BlockSpec((1, LANES), lambda i: (0, 0)),
        ],
        out_specs=[
            pl.BlockSpec((tm, d), lambda i: (i, 0)),
            pl.BlockSpec((tm, LANES), lambda i: (i, 0)),
        ],
        out_shape=[
            jax.ShapeDtypeStruct((n, d), F32),
            jax.ShapeDtypeStruct((n, LANES), F32),
        ],
        compiler_params=_params(("arbitrary",)),
    )(x2, g.reshape(1, d), mod_l, w_router16, b_router)


GATHER_UNROLL = 8


def _start_row_gather(idx_ref, base, n_rows, src_hbm, dst, sem):
    def body(r, c):
        row = idx_ref[base + r]
        pltpu.make_async_copy(src_hbm.at[pl.ds(row, 1), :], dst.at[pl.ds(r, 1), :], sem).start()
        return c
    lax.fori_loop(0, n_rows, body, 0, unroll=GATHER_UNROLL)


def _wait_row_gather(src_hbm, dst, sem):
    pltpu.make_async_copy(src_hbm.at[pl.ds(0, dst.shape[0]), :], dst, sem).wait()


def _ffn_kernel(tile_exp_ref, n_tiles_ref, src_ref, h_hbm, w1_ref, w3_ref, w2_ref, y_ref, xbuf, sem, *, tm):
    t = pl.program_id(0)
    n_tiles = n_tiles_ref[0]
    slot = t % 2

    def start_tile(tile, s):
        _start_row_gather(src_ref, tile * tm, tm, h_hbm, xbuf.at[s], sem.at[s])

    @pl.when(jnp.logical_and(t == 0, n_tiles > 0))
    def _():
        start_tile(0, 0)

    @pl.when(t + 1 < n_tiles)
    def _():
        start_tile(t + 1, 1 - slot)

    @pl.when(t < n_tiles)
    def _():
        _wait_row_gather(h_hbm, xbuf.at[slot], sem.at[slot])
        x = xbuf[slot].astype(BF16)
        a = jnp.dot(x, w1_ref[...], preferred_element_type=F32)
        b = jnp.dot(x, w3_ref[...], preferred_element_type=F32)
        hid = (a * _sigmoid(a) * b).astype(BF16)
        y_ref[...] = jnp.dot(hid, w2_ref[...], preferred_element_type=F32)

    @pl.when(t >= n_tiles)
    def _():
        y_ref[...] = jnp.zeros_like(y_ref)


def _ffn(h2, tile_expert, n_tiles, src_token, w1, w3, w2, layer, tm):
    n, d = h2.shape
    de = w1.shape[3]
    max_tiles = tile_expert.shape[0]
    grid_spec = pltpu.PrefetchScalarGridSpec(
        num_scalar_prefetch=3,
        grid=(max_tiles,),
        in_specs=[
            pl.BlockSpec(memory_space=pl.ANY),
            pl.BlockSpec((None, None, d, de), lambda t, te, nt, src: (layer, te[t], 0, 0)),
            pl.BlockSpec((None, None, d, de), lambda t, te, nt, src: (layer, te[t], 0, 0)),
            pl.BlockSpec((None, None, de, d), lambda t, te, nt, src: (layer, te[t], 0, 0)),
        ],
        out_specs=pl.BlockSpec((tm, d), lambda t, te, nt, src: (t, 0)),
        scratch_shapes=[
            pltpu.VMEM((2, tm, d), F32),
            pltpu.SemaphoreType.DMA((2,)),
        ],
    )
    return pl.pallas_call(
        functools.partial(_ffn_kernel, tm=tm),
        name="moe_ffn",
        grid_spec=grid_spec,
        out_shape=jax.ShapeDtypeStruct((max_tiles * tm, d), F32),
        compiler_params=_params(("arbitrary",), disable_bounds_checks=True),
    )(tile_expert, n_tiles, src_token, h2, w1, w3, w2)


def _combine_kernel(pos_ref, y_hbm, x_ref, mod_ref, w_ref, o_ref, ybuf, sem, *, tc):
    t = pl.program_id(0)
    n_steps = pl.num_programs(0)
    slot = t % 2

    def start_step(step, s):
        _start_row_gather(pos_ref, step * (TOP_K * tc), TOP_K * tc, y_hbm, ybuf.at[s], sem.at[s])

    @pl.when(t == 0)
    def _():
        start_step(0, 0)

    @pl.when(t + 1 < n_steps)
    def _():
        start_step(t + 1, 1 - slot)

    _wait_row_gather(y_hbm, ybuf.at[slot], sem.at[slot])
    w = w_ref[...]
    moe = w[:, 2:3] * ybuf[slot, 0:tc, :] + w[:, 3:4] * ybuf[slot, tc:2 * tc, :]
    o_ref[...] = x_ref[...] + mod_ref[5:6, :] * moe


def _combine(pos_steps, y_sorted, x2, mod_l, route, seq, tc):
    n, d = x2.shape
    per_seq = seq // tc
    grid_spec = pltpu.PrefetchScalarGridSpec(
        num_scalar_prefetch=1,
        grid=(n // tc,),
        in_specs=[
            pl.BlockSpec(memory_space=pl.ANY),
            pl.BlockSpec((tc, d), lambda t, pos: (t, 0)),
            pl.BlockSpec((None, 6, d), lambda t, pos: (t // per_seq, 0, 0)),
            pl.BlockSpec((tc, LANES), lambda t, pos: (t, 0)),
        ],
        out_specs=pl.BlockSpec((tc, d), lambda t, pos: (t, 0)),
        scratch_shapes=[
            pltpu.VMEM((2, TOP_K * tc, d), F32),
            pltpu.SemaphoreType.DMA((2,)),
        ],
    )
    return pl.pallas_call(
        functools.partial(_combine_kernel, tc=tc),
        name="moe_combine",
        grid_spec=grid_spec,
        out_shape=jax.ShapeDtypeStruct((n, d), F32),
        compiler_params=_params(("arbitrary",), disable_bounds_checks=True),
    )(pos_steps, y_sorted, x2, mod_l, route)


def _moe_plan(route, n_exp, tm, tc):
    n = route.shape[0]
    eid = route[:, 0:TOP_K].astype(jnp.int32)
    flat_e = eid.reshape(-1)
    onehot = (flat_e[:, None] == jnp.arange(n_exp, dtype=jnp.int32)[None, :]).astype(jnp.int32)
    csum = jnp.cumsum(onehot, axis=0)
    rank = jnp.sum(csum * onehot, axis=1) - 1
    counts = csum[-1]
    tiles = (counts + tm - 1) // tm
    tile_end = jnp.cumsum(tiles)
    tile_start = tile_end - tiles
    n_tiles = tile_end[-1]
    pos = (tile_start * tm)[flat_e] + rank
    max_tiles = (TOP_K * n) // tm + n_exp
    token = jnp.arange(TOP_K * n, dtype=jnp.int32) // TOP_K
    src_token = jnp.zeros((max_tiles * tm,), jnp.int32).at[pos].set(token)
    tix = jnp.minimum(jnp.arange(max_tiles, dtype=jnp.int32), n_tiles - 1)
    tile_expert = jnp.searchsorted(tile_end, tix, side="right").astype(jnp.int32)
    pos_steps = pos.reshape(n // tc, tc, TOP_K).transpose(0, 2, 1).reshape(-1)
    return tile_expert, n_tiles.reshape(1).astype(jnp.int32), src_token, pos_steps.astype(jnp.int32)


def kernel(x, c, positions, ada_w, ada_b, norm1_g, norm2_g, w_in, b_gate, lru_conv_w, lru_conv_b, lru_wr, lru_br, lru_wi, lru_bi, lru_lambda, pool_w, pool_b, pool_scale, q_norm_g, k_norm_g, lam_q1, lam_k1, lam_q2, lam_k2, subln_g, cv_dw_w, cv_dw_b, cv_ln_g, cv_ln_b, w_branch, w_out, router_g_w, router_g_b, router_e_w, router_e_b, moe_w1, moe_w3, moe_w2):
    bsz, seq, d = x.shape
    depth = ada_w.shape[0]
    db = lru_conv_w.shape[2]
    n_branch = w_branch.shape[1]
    n_mix = w_in.shape[2] - n_branch * d
    assert n_mix == 8 * db and q_norm_g.shape[1] == HEAD_DIM
    n_groups = router_g_w.shape[2]
    n_exp = router_e_w.shape[2]
    per_group = n_exp // n_groups
    assert n_groups + n_exp <= LANES
    n = bsz * seq
    col = {name: idx for idx, name in enumerate(("lru_x", "lru_gate", "pool", "q", "k", "v", "glu_val", "glu_gate"))}

    mod = _ada_mod(c, ada_w, ada_b).reshape(depth, bsz, 6, d)
    cos_t, sa_t, sb_t = _rope_tables(positions)
    seg = jnp.kron(jnp.eye(HEAD_W // HEAD_DIM, dtype=F32), jnp.full((HEAD_DIM, HEAD_DIM), 1.0 / HEAD_DIM, F32)).astype(BF16)
    dup = lambda g: jnp.concatenate([g, g]).reshape(1, HEAD_W)
    ffn_tm = _tile(TOP_K * n, 256)
    comb_tc = _tile(seq, 128)

    w_in16, w_branch16, w_out16 = w_in.astype(BF16), w_branch.astype(BF16), w_out.astype(BF16)
    moe_w1_16, moe_w3_16, moe_w2_16 = moe_w1.astype(BF16), moe_w3.astype(BF16), moe_w2.astype(BF16)

    x2 = x.reshape(n, d)
    for l in range(depth):
        mod_l = mod[l]
        h = _norm(x2, norm1_g[l], mod_l, seq, 0, 1)
        proj = _matmul(h, w_in16, l, n_mix)
        y_a = _lru_branch(proj, bsz, seq, col["lru_x"], col["lru_gate"], lru_conv_w[l], lru_conv_b[l],
                          lru_wr[l], lru_br[l], lru_wi[l], lru_bi[l], lru_lambda[l])
        y_b = _pool_branch(proj, bsz, seq, col["pool"], pool_w[l], pool_b[l], pool_scale[l])
        qn, kt = _qk_prep(proj, bsz, seq, col["q"], col["k"], db, cos_t, sa_t, sb_t, seg,
                          dup(q_norm_g[l]), dup(k_norm_g[l]))
        lambda_init = 0.8 - 0.6 * math.exp(-0.3 * l)
        y_c = _flash(qn, kt, proj, bsz, seq, col["v"], lam_q1[l], lam_k1[l], lam_q2[l], lam_k2[l],
                     subln_g[l], lambda_init)
        y_d = _conv_branch(proj, bsz, seq, col["glu_val"], col["glu_gate"], cv_dw_w[l], cv_dw_b[l],
                           cv_ln_g[l], cv_ln_b[l])
        merged = _merge(h, (y_a, y_b, y_c, y_d), w_in16, l, n_mix, b_gate[l], w_branch16)
        x2 = _resid_matmul(merged, w_out16, l, x2, mod_l, seq, 2)

        w_router = jnp.concatenate([router_g_w[l], router_e_w[l]], axis=1)
        w_router = jnp.pad(w_router, ((0, 0), (0, LANES - w_router.shape[1]))).astype(BF16)
        b_router = jnp.pad(jnp.concatenate([router_g_b[l], router_e_b[l]]), (0, LANES - n_groups - n_exp)).reshape(1, LANES)
        h2, route = _router(x2, norm2_g[l], mod_l, seq, w_router, b_router, n_groups, per_group)
        tile_expert, n_tiles, src_token, pos_steps = _moe_plan(route, n_exp, ffn_tm, comb_tc)
        y_sorted = _ffn(h2, tile_expert, n_tiles, src_token, moe_w1_16, moe_w3_16, moe_w2_16, l, ffn_tm)
        x2 = _combine(pos_steps, y_sorted, x2, mod_l, route, seq, comb_tc)
    return x2.reshape(bsz, seq, d)
```

```python
import functools
import math

import jax
import jax.numpy as jnp
from jax import lax
from jax.experimental import pallas as pl
from jax.experimental.pallas import tpu as pltpu

EPS = 1e-6
LRU_C = 8.0
LRU_CONV_TAPS = 4
POOL_WINDOWS = (2, 4, 8, 16)
POOL_HALO = 16
CONV_HALO = 32
HEAD_DIM = 64
HEAD_W = 2 * HEAD_DIM
ROPE_DIM = HEAD_DIM // 4
ROPE_HALF = ROPE_DIM // 2
ROPE_THETA = 500000.0
TOP_K = 2
NEG_BIG = -1e30
LOG2_E = math.log2(math.e)
SUBLANES = 8
LANES = 128
VMEM_LIMIT_BYTES = 56 * 1024 * 1024

BF16 = jnp.bfloat16
F32 = jnp.float32


def _tile(n, pref):
    t = min(n, pref)
    assert n % t == 0, (n, pref)
    return t


def _params(sem, **kw):
    return pltpu.CompilerParams(dimension_semantics=sem, vmem_limit_bytes=VMEM_LIMIT_BYTES, **kw)


def _sigmoid(x):
    return 1.0 / (1.0 + jnp.exp(-x))


def _ada_kernel(c_ref, w_ref, b_ref, o_ref):
    c = c_ref[...]
    c_act = (c * _sigmoid(c)).astype(BF16)
    o_ref[...] = jnp.dot(c_act, w_ref[...].astype(BF16), preferred_element_type=F32) + b_ref[...]


def _ada_mod(c, ada_w, ada_b):
    depth, d, n_out = ada_w.shape
    bsz = c.shape[0]
    tn = _tile(n_out, 1024)
    return pl.pallas_call(
        _ada_kernel,
        name="ada_mod",
        grid=(depth, n_out // tn),
        in_specs=[
            pl.BlockSpec((bsz, d), lambda l, j: (0, 0)),
            pl.BlockSpec((None, d, tn), lambda l, j: (l, 0, j)),
            pl.BlockSpec((None, 1, tn), lambda l, j: (l, 0, j)),
        ],
        out_specs=pl.BlockSpec((None, bsz, tn), lambda l, j: (l, 0, j)),
        out_shape=jax.ShapeDtypeStruct((depth, bsz, n_out), F32),
        compiler_params=_params(("arbitrary", "arbitrary")),
    )(c, ada_w, ada_b.reshape(depth, 1, n_out))


def _modnorm(x, g, mod, shift_row, scale_row):
    y = x * lax.rsqrt(jnp.mean(x * x, axis=-1, keepdims=True) + EPS) * g
    return y * (1.0 + mod[scale_row:scale_row + 1, :]) + mod[shift_row:shift_row + 1, :]


def _norm_kernel(x_ref, g_ref, mod_ref, o_ref, *, shift_row, scale_row):
    o_ref[...] = _modnorm(x_ref[...], g_ref[...], mod_ref[...], shift_row, scale_row).astype(o_ref.dtype)


def _norm(x2, g, mod_l, seq, shift_row, scale_row):
    n, d = x2.shape
    tm = _tile(seq, 512)
    per_seq = seq // tm
    return pl.pallas_call(
        functools.partial(_norm_kernel, shift_row=shift_row, scale_row=scale_row),
        name="mod_norm",
        grid=(n // tm,),
        in_specs=[
            pl.BlockSpec((tm, d), lambda i: (i, 0)),
            pl.BlockSpec((1, d), lambda i: (0, 0)),
            pl.BlockSpec((None, 6, d), lambda i: (i // per_seq, 0, 0)),
        ],
        out_specs=pl.BlockSpec((tm, d), lambda i: (i, 0)),
        out_shape=jax.ShapeDtypeStruct((n, d), BF16),
        compiler_params=_params(("arbitrary",)),
    )(x2, g.reshape(1, d), mod_l)


def _matmul_kernel(a_ref, w_ref, o_ref):
    o_ref[...] = jnp.dot(a_ref[...], w_ref[...], preferred_element_type=F32).astype(o_ref.dtype)


def _matmul(a, w, layer, n_cols):
    m, k = a.shape
    tm = _tile(m, 1024)
    tn = _tile(n_cols, 1024)
    return pl.pallas_call(
        _matmul_kernel,
        name="in_proj",
        grid=(m // tm, n_cols // tn),
        in_specs=[
            pl.BlockSpec((tm, k), lambda i, j: (i, 0)),
            pl.BlockSpec((None, k, tn), lambda i, j: (layer, 0, j)),
        ],
        out_specs=pl.BlockSpec((tm, tn), lambda i, j: (i, j)),
        out_shape=jax.ShapeDtypeStruct((m, n_cols), BF16),
        compiler_params=_params(("arbitrary", "arbitrary")),
    )(a, w)


def _gelu_tanh(x):
    return 0.5 * x * (1.0 + jnp.tanh(math.sqrt(2.0 / math.pi) * (x + 0.044715 * (x * x * x))))


def _lru_kernel(x_ref, gate_ref, cw_ref, cb_ref, wr_ref, br_ref, wi_ref, bi_ref, lam_ref, o_ref,
                xe_ref, a_ref, u_ref, carry_ref):
    ts, db = x_ref.shape
    n_blocks, blk = wr_ref.shape[0], wr_ref.shape[1]
    halo = SUBLANES
    first = pl.program_id(1) == 0

    @pl.when(first)
    def _():
        xe_ref[0:halo, :] = jnp.zeros((halo, db), F32)
        carry_ref[...] = jnp.zeros_like(carry_ref)

    @pl.when(jnp.logical_not(first))
    def _():
        xe_ref[0:halo, :] = xe_ref[ts:ts + halo, :]

    xe_ref[halo:halo + ts, :] = x_ref[...].astype(F32)

    xa = jnp.zeros((ts, db), F32) + cb_ref[...]
    for j in range(LRU_CONV_TAPS):
        off = halo - (LRU_CONV_TAPS - 1) + j
        xa = xa + cw_ref[j:j + 1, :] * xe_ref[off:off + ts, :]

    xa16 = xa.astype(BF16)
    for hb in range(n_blocks):
        sl = slice(hb * blk, (hb + 1) * blk)
        a_ref[:, sl] = jnp.dot(xa16[:, sl], wr_ref[hb], preferred_element_type=F32)
        u_ref[:, sl] = jnp.dot(xa16[:, sl], wi_ref[hb], preferred_element_type=F32)
    r = _sigmoid(a_ref[...] + br_ref[...])
    ig = _sigmoid(u_ref[...] + bi_ref[...])
    z = -lam_ref[...]
    softplus = jnp.maximum(z, 0.0) + jnp.log(1.0 + jnp.exp(-jnp.abs(z)))
    a = jnp.exp((-LRU_C) * r * softplus)
    u = jnp.sqrt(1.0 - a * a) * (ig * xa)

    row = lax.broadcasted_iota(jnp.int32, (ts, db), 0) % SUBLANES
    for d in (1, 2, 4):
        keep = row >= d
        a_sh = jnp.where(keep, pltpu.roll(a, d, 0), 1.0)
        u_sh = jnp.where(keep, pltpu.roll(u, d, 0), 0.0)
        u = a * u_sh + u
        a = a * a_sh
    a_ref[...] = a
    u_ref[...] = u

    def body(c, h_prev):
        r0 = pl.multiple_of(c * SUBLANES, SUBLANES)
        h8 = u_ref[pl.ds(r0, SUBLANES), :] + a_ref[pl.ds(r0, SUBLANES), :] * h_prev
        u_ref[pl.ds(r0, SUBLANES), :] = h8
        return h8[SUBLANES - 1:SUBLANES, :]

    carry_ref[...] = lax.fori_loop(0, ts // SUBLANES, body, carry_ref[...])
    o_ref[...] = (_gelu_tanh(gate_ref[...].astype(F32)) * u_ref[...]).astype(o_ref.dtype)


def _lru_branch(proj, bsz, seq, col_x, col_gate, cw, cb, wr, br, wi, bi, lam):
    db = cw.shape[1]
    ts = _tile(seq, 256)
    per_seq = seq // ts
    n_blocks, blk = wr.shape[0], wr.shape[1]
    row = lambda v: v.reshape(1, db)
    vec = pl.BlockSpec((1, db), lambda b, i: (0, 0))
    wspec = pl.BlockSpec((n_blocks, blk, blk), lambda b, i: (0, 0, 0))
    return pl.pallas_call(
        _lru_kernel,
        name="lru_branch",
        grid=(bsz, per_seq),
        in_specs=[
            pl.BlockSpec((ts, db), lambda b, i: (b * per_seq + i, col_x)),
            pl.BlockSpec((ts, db), lambda b, i: (b * per_seq + i, col_gate)),
            pl.BlockSpec((LRU_CONV_TAPS, db), lambda b, i: (0, 0)),
            vec, wspec, vec, wspec, vec, vec,
        ],
        out_specs=pl.BlockSpec((ts, db), lambda b, i: (b * per_seq + i, 0)),
        out_shape=jax.ShapeDtypeStruct((bsz * seq, db), BF16),
        scratch_shapes=[
            pltpu.VMEM((ts + SUBLANES, db), F32),
            pltpu.VMEM((ts, db), F32),
            pltpu.VMEM((ts, db), F32),
            pltpu.VMEM((1, db), F32),
        ],
        compiler_params=_params(("arbitrary", "arbitrary")),
    )(proj, proj, cw, row(cb), wr.astype(BF16), row(br), wi.astype(BF16), row(bi), row(lam))


def _pool_kernel(x_ref, w_ref, b_ref, s_ref, o_ref, xe_ref):
    ts, db = x_ref.shape
    n_groups, pg = w_ref.shape[0], w_ref.shape[1]
    halo = POOL_HALO
    i = pl.program_id(1)

    @pl.when(i == 0)
    def _():
        xe_ref[0:halo, :] = jnp.zeros((halo, db), F32)

    @pl.when(i != 0)
    def _():
        xe_ref[0:halo, :] = xe_ref[ts:ts + halo, :]

    xe_ref[halo:halo + ts, :] = x_ref[...].astype(F32)
    t1 = (i * ts + 1 + lax.broadcasted_iota(jnp.int32, (ts, pg), 0)).astype(F32)
    for gi in range(n_groups):
        win = POOL_WINDOWS[gi]
        sl = slice(gi * pg, (gi + 1) * pg)
        x = xe_ref[halo:halo + ts, sl]
        acc = x
        for j in range(1, win):
            acc = acc + xe_ref[halo - j:halo - j + ts, sl]
        diff = acc / jnp.minimum(t1, float(win)) - x
        y = jnp.dot(diff.astype(BF16), w_ref[gi], preferred_element_type=F32) + b_ref[:, sl]
        o_ref[:, sl] = (y * s_ref[:, sl]).astype(o_ref.dtype)


def _pool_branch(proj, bsz, seq, col, w, b, scale):
    n_groups, pg = w.shape[0], w.shape[1]
    assert n_groups == len(POOL_WINDOWS)
    db = n_groups * pg
    ts = _tile(seq, 512)
    per_seq = seq // ts
    vec = pl.BlockSpec((1, db), lambda bb, i: (0, 0))
    return pl.pallas_call(
        _pool_kernel,
        name="pool_branch",
        grid=(bsz, per_seq),
        in_specs=[
            pl.BlockSpec((ts, db), lambda bb, i: (bb * per_seq + i, col)),
            pl.BlockSpec((n_groups, pg, pg), lambda bb, i: (0, 0, 0)),
            vec, vec,
        ],
        out_specs=pl.BlockSpec((ts, db), lambda bb, i: (bb * per_seq + i, 0)),
        out_shape=jax.ShapeDtypeStruct((bsz * seq, db), BF16),
        scratch_shapes=[pltpu.VMEM((ts + POOL_HALO, db), F32)],
        compiler_params=_params(("arbitrary", "arbitrary")),
    )(proj, w.astype(BF16), b.reshape(1, db), scale.reshape(1, db))


def _conv_kernel(val_ref, gate_ref, w_ref, b_ref, g_ref, beta_ref, o_ref, ue_ref, sh_ref, *, row_block):
    ts, db = val_ref.shape
    taps = w_ref.shape[0]
    halo = CONV_HALO
    i = pl.program_id(1)

    @pl.when(i == 0)
    def _():
        ue_ref[0:halo, :] = jnp.zeros((halo, db), F32)

    @pl.when(i != 0)
    def _():
        ue_ref[0:halo, :] = ue_ref[ts:ts + halo, :]

    ue_ref[halo:halo + ts, :] = val_ref[...].astype(F32) * _sigmoid(gate_ref[...].astype(F32))
    span = sh_ref.shape[1]
    for p in range(1, SUBLANES):
        sh_ref[p - 1] = ue_ref[p:p + span, :]
    base = halo - (taps - 1)
    for rb in range(ts // row_block):
        r0 = rb * row_block
        acc = jnp.zeros((row_block, db), F32) + b_ref[...]
        for j in range(taps):
            a, p = divmod(base + j, SUBLANES)
            row = r0 + a * SUBLANES
            window = ue_ref[row:row + row_block, :] if p == 0 else sh_ref[p - 1, row:row + row_block, :]
            acc = acc + w_ref[j:j + 1, :] * window
        mu = jnp.mean(acc, axis=-1, keepdims=True)
        cen = acc - mu
        var = jnp.mean(cen * cen, axis=-1, keepdims=True)
        y = cen * lax.rsqrt(var + EPS) * g_ref[...] + beta_ref[...]
        o_ref[r0:r0 + row_block, :] = (y * _sigmoid(y)).astype(o_ref.dtype)


def _conv_branch(proj, bsz, seq, col_val, col_gate, w, b, g, beta):
    taps, db = w.shape
    assert taps - 1 <= CONV_HALO
    ts = _tile(seq, 256)
    per_seq = seq // ts
    row = lambda v: v.reshape(1, db)
    vec = pl.BlockSpec((1, db), lambda bb, i: (0, 0))
    return pl.pallas_call(
        functools.partial(_conv_kernel, row_block=_tile(ts, 32)),
        name="conv_branch",
        grid=(bsz, per_seq),
        in_specs=[
            pl.BlockSpec((ts, db), lambda bb, i: (bb * per_seq + i, col_val)),
            pl.BlockSpec((ts, db), lambda bb, i: (bb * per_seq + i, col_gate)),
            pl.BlockSpec((taps, db), lambda bb, i: (0, 0)),
            vec, vec, vec,
        ],
        out_specs=pl.BlockSpec((ts, db), lambda bb, i: (bb * per_seq + i, 0)),
        out_shape=jax.ShapeDtypeStruct((bsz * seq, db), BF16),
        scratch_shapes=[
            pltpu.VMEM((ts + CONV_HALO, db), F32),
            pltpu.VMEM((SUBLANES - 1, ts + CONV_HALO - SUBLANES, db), F32),
        ],
        compiler_params=_params(("arbitrary", "arbitrary")),
    )(proj, proj, w, row(b), row(g), row(beta))


def _qk_prep_kernel(q_ref, k_ref, cos_ref, sa_ref, sb_ref, seg_ref, qg_ref, kg_ref, qo_ref, kt_ref, kn_ref):
    ts, db = q_ref.shape
    cos, sa, sb = cos_ref[...], sa_ref[...], sb_ref[...]

    def norm_rope(x, g):
        ms = jnp.dot((x * x).astype(BF16), seg_ref[...], preferred_element_type=F32)
        xn = x * lax.rsqrt(ms + EPS) * g
        return (xn * cos + pltpu.roll(xn, HEAD_W - ROPE_HALF, 1) * sa + pltpu.roll(xn, ROPE_HALF, 1) * sb)

    for h in range(db // HEAD_W):
        sl = slice(h * HEAD_W, (h + 1) * HEAD_W)
        q = norm_rope(q_ref[:, sl].astype(F32), qg_ref[...])
        qo_ref[:, sl] = (q * (LOG2_E / math.sqrt(HEAD_DIM))).astype(qo_ref.dtype)
        kn_ref[:, sl] = norm_rope(k_ref[:, sl].astype(F32), kg_ref[...])
    kt_ref[...] = kn_ref[...].T.astype(kt_ref.dtype)


def _qk_prep(proj, bsz, seq, col_q, col_k, db, cos_t, sa_t, sb_t, seg, qg, kg):
    ts = _tile(seq, 256)
    per_seq = seq // ts
    tab = pl.BlockSpec((ts, HEAD_W), lambda b, i: (b * per_seq + i, 0))
    vec = pl.BlockSpec((1, HEAD_W), lambda b, i: (0, 0))
    return pl.pallas_call(
        _qk_prep_kernel,
        name="qk_prep",
        grid=(bsz, per_seq),
        in_specs=[
            pl.BlockSpec((ts, db), lambda b, i: (b * per_seq + i, col_q)),
            pl.BlockSpec((ts, db), lambda b, i: (b * per_seq + i, col_k)),
            tab, tab, tab,
            pl.BlockSpec((HEAD_W, HEAD_W), lambda b, i: (0, 0)),
            vec, vec,
        ],
        out_specs=[
            pl.BlockSpec((ts, db), lambda b, i: (b * per_seq + i, 0)),
            pl.BlockSpec((None, db, ts), lambda b, i: (b, 0, i)),
        ],
        out_shape=[
            jax.ShapeDtypeStruct((bsz * seq, db), BF16),
            jax.ShapeDtypeStruct((bsz, db, seq), BF16),
        ],
        scratch_shapes=[pltpu.VMEM((ts, db), F32)],
        compiler_params=_params(("arbitrary", "arbitrary")),
    )(proj, proj, cos_t, sa_t, sb_t, seg, qg, kg)


def _flash_kernel(q_ref, kt_ref, v_ref, lq1_ref, lk1_ref, lq2_ref, lk2_ref, sg_ref, o_ref,
                  m_ref, l_ref, acc_ref, p_ref, *, lambda_init, tk):
    tq = q_ref.shape[0]
    rows = 2 * tq
    i = pl.program_id(2)
    lane = lax.broadcasted_iota(jnp.int32, (tq, HEAD_W), 1)
    q = q_ref[...]
    zero = jnp.zeros_like(q)
    q2 = jnp.concatenate([jnp.where(lane < HEAD_DIM, q, zero), jnp.where(lane >= HEAD_DIM, q, zero)], axis=0)
    m_ref[...] = jnp.full(m_ref.shape, NEG_BIG, F32)
    l_ref[...] = jnp.zeros(l_ref.shape, F32)
    acc_ref[...] = jnp.zeros(acc_ref.shape, F32)
    p_ref[...] = jnp.zeros(p_ref.shape, p_ref.dtype)

    def pv(j):
        k0 = pl.multiple_of(j * tk, tk)
        return jnp.dot(p_ref[...], v_ref[pl.ds(k0, tk), :], preferred_element_type=F32)

    n_full = (i * tq) // tk
    n_all = ((i + 1) * tq + tk - 1) // tk

    def step(j, masked):
        k0 = pl.multiple_of(j * tk, tk)
        s = jnp.dot(q2, kt_ref[:, pl.ds(k0, tk)], preferred_element_type=F32)
        pv_prev = pv(jnp.maximum(j - 1, 0))
        if masked:
            qpos = i * tq + lax.broadcasted_iota(jnp.int32, (rows, tk), 0) % tq
            kpos = k0 + lax.broadcasted_iota(jnp.int32, (rows, tk), 1)
            s = jnp.where(kpos <= qpos, s, NEG_BIG)
        cols = [s[:, cc * LANES:(cc + 1) * LANES] for cc in range(tk // LANES)]
        m_prev = m_ref[...]
        m_next = jnp.maximum(m_prev, jnp.max(functools.reduce(jnp.maximum, cols), axis=-1, keepdims=True))
        alpha = jnp.exp2(m_prev - m_next)
        ps = [jnp.exp2(col - m_next) for col in cols]
        l_ref[...] = alpha * l_ref[...] + functools.reduce(jnp.add, ps)
        acc_ref[...] = alpha * (acc_ref[...] + pv_prev)
        p_ref[...] = jnp.concatenate([x.astype(p_ref.dtype) for x in ps], axis=1)
        m_ref[...] = m_next

    def full_body(j, c):
        step(j, False)
        return c

    def diag_body(j, c):
        step(j, True)
        return c

    lax.fori_loop(0, n_full, full_body, 0)
    lax.fori_loop(n_full, n_all, diag_body, 0)

    lam = (jnp.exp(jnp.sum(lq1_ref[...] * lk1_ref[...], axis=-1, keepdims=True))
           - jnp.exp(jnp.sum(lq2_ref[...] * lk2_ref[...], axis=-1, keepdims=True)) + lambda_init)
    o_all = (acc_ref[...] + pv(n_all - 1)) / jnp.sum(l_ref[...], axis=-1, keepdims=True)
    o = o_all[0:tq, :] - lam * o_all[tq:rows, :]
    o = o * lax.rsqrt(jnp.mean(o * o, axis=-1, keepdims=True) + EPS) * sg_ref[...]
    o_ref[...] = (o * (1.0 - lambda_init)).astype(o_ref.dtype)


def _flash(qn, kt, proj, bsz, seq, col_v, lq1, lk1, lq2, lk2, sub_g, lambda_init):
    db = qn.shape[1]
    heads = db // HEAD_W
    tq = _tile(seq, 512)
    tk = _tile(seq, 512)
    per_seq = seq // tq
    vec64 =pl.BlockSpec((1, HEAD_DIM), lambda b, h, i: (0, 0))
    row64 = lambda v: v.reshape(1, HEAD_DIM)
    return pl.pallas_call(
        functools.partial(_flash_kernel, lambda_init=lambda_init, tk=tk),
        name="diff_attn",
        grid=(bsz, heads, per_seq),
        in_specs=[
            pl.BlockSpec((tq, HEAD_W), lambda b, h, i: (b * per_seq + i, h)),
            pl.BlockSpec((None, HEAD_W, seq), lambda b, h, i: (b, h, 0)),
            pl.BlockSpec((seq, HEAD_W), lambda b, h, i: (b, col_v * heads + h)),
            vec64, vec64, vec64, vec64,
            pl.BlockSpec((1, HEAD_W), lambda b, h, i: (0, 0)),
        ],
        out_specs=pl.BlockSpec((tq, HEAD_W), lambda b, h, i: (b * per_seq + i, h)),
        out_shape=jax.ShapeDtypeStruct((bsz * seq, db), BF16),
        scratch_shapes=[pltpu.VMEM((2 * tq, HEAD_W), F32)] * 3 + [pltpu.VMEM((2 * tq, tk), BF16)],
        compiler_params=_params(("arbitrary", "arbitrary", "arbitrary")),
    )(qn, kt, proj, row64(lq1), row64(lk1), row64(lq2), row64(lk2), sub_g.reshape(1, HEAD_W))


def _rope_tables(positions):
    inv_freq = ROPE_THETA ** (-jnp.arange(0, ROPE_DIM, 2, dtype=F32) / ROPE_DIM)
    ang = positions.astype(F32).reshape(-1, 1) * inv_freq
    cos, sin = jnp.cos(ang), jnp.sin(ang)
    n = ang.shape[0]
    rest = HEAD_DIM - ROPE_DIM
    zeros_h = jnp.zeros((n, ROPE_HALF), F32)
    cos_s = jnp.concatenate([cos, cos, jnp.ones((n, rest), F32)], axis=1)
    sa_s = jnp.concatenate([-sin, zeros_h, jnp.zeros((n, rest), F32)], axis=1)
    sb_s = jnp.concatenate([zeros_h, sin, jnp.zeros((n, rest), F32)], axis=1)
    rep = lambda t: jnp.concatenate([t, t], axis=1)
    return rep(cos_s), rep(sa_s), rep(sb_s)


def _merge_kernel(h_ref, ya_ref, yb_ref, yc_ref, yd_ref, wa_ref, wb_ref, wc_ref, wd_ref,
                  bg_ref, wbr_ref, o_ref):
    h = h_ref[...]
    acc = None
    for bi, (y_ref, wg_ref) in enumerate(((ya_ref, wa_ref), (yb_ref, wb_ref), (yc_ref, wc_ref), (yd_ref, wd_ref))):
        gate = _sigmoid(jnp.dot(h, wg_ref[...], preferred_element_type=F32) + bg_ref[bi:bi + 1, :])
        term = gate * jnp.dot(y_ref[...], wbr_ref[bi], preferred_element_type=F32)
        acc = term if acc is None else acc + term
    o_ref[...] = acc.astype(o_ref.dtype)


def _merge(h, ys, w_in16, layer, n_mix, b_gate, w_branch16):
    n, d = h.shape
    db = ys[0].shape[1]
    n_branch = len(ys)
    tm = _tile(n, 512)
    tn = _tile(d, 256)
    gate_blk = [(n_mix + b * d) // tn for b in range(n_branch)]
    wspecs = [pl.BlockSpec((None, d, tn), functools.partial(lambda j, i, base: (layer, 0, base + j), base=gb))
              for gb in gate_blk]
    yspec = pl.BlockSpec((tm, db), lambda j, i: (i, 0))
    return pl.pallas_call(
        _merge_kernel,
        name="gated_merge",
        grid=(d // tn, n // tm),
        in_specs=[pl.BlockSpec((tm, d), lambda j, i: (i, 0))] + [yspec] * n_branch + wspecs + [
            pl.BlockSpec((n_branch, tn), lambda j, i: (0, j)),
            pl.BlockSpec((None, n_branch, db, tn), lambda j, i: (layer, 0, 0, j)),
        ],
        out_specs=pl.BlockSpec((tm, tn), lambda j, i: (i, j)),
        out_shape=jax.ShapeDtypeStruct((n, d), BF16),
        compiler_params=_params(("arbitrary", "arbitrary")),
    )(h, *ys, *([w_in16] * n_branch), b_gate, w_branch16)


def _resid_matmul_kernel(a_ref, w_ref, x_ref, mod_ref, o_ref, *, gate_row):
    y = jnp.dot(a_ref[...], w_ref[...], preferred_element_type=F32)
    o_ref[...] = x_ref[...] + mod_ref[gate_row:gate_row + 1, :] * y


def _resid_matmul(a, w16, layer, x2, mod_l, seq, gate_row):
    n, k = a.shape
    d = w16.shape[2]
    tm = _tile(seq, 1024)
    tn = _tile(d, 512)
    per_seq = seq // tm
    return pl.pallas_call(
        functools.partial(_resid_matmul_kernel, gate_row=gate_row),
        name="out_proj_resid",
        grid=(n // tm, d // tn),
        in_specs=[
            pl.BlockSpec((tm, k), lambda i, j: (i, 0)),
            pl.BlockSpec((None, k, tn), lambda i, j: (layer, 0, j)),
            pl.BlockSpec((tm, tn), lambda i, j: (i, j)),
            pl.BlockSpec((None, 6, tn), lambda i, j: (i // per_seq, 0, j)),
        ],
        out_specs=pl.BlockSpec((tm, tn), lambda i, j: (i, j)),
        out_shape=jax.ShapeDtypeStruct((n, d), F32),
        compiler_params=_params(("arbitrary", "arbitrary")),
    )(a, w16, x2, mod_l)


def _pack_halves(v):
    w = v.shape[1] // 2
    bits = lambda t: pltpu.bitcast(t.astype(BF16).astype(F32), jnp.uint32)
    return bits(v[:, w:]) | (bits(v[:, :w]) >> 16)


def _unpack_halves(p):
    return (pltpu.bitcast(p << 16, F32), pltpu.bitcast(p & jnp.uint32(0xFFFF0000), F32))


def _store_tiled_rows(ref3, val2):
    for s in range(ref3.shape[0]):
        ref3[s] = val2[:, s * LANES:(s + 1) * LANES]


def _load_tiled_rows(ref3):
    return jnp.concatenate([ref3[s] for s in range(ref3.shape[0])], axis=1)


def _router_kernel(x_ref, g_ref, mod_ref, wr_ref, br_ref, h_ref, r_ref, *, n_groups, per_group):
    h2 = _modnorm(x_ref[...], g_ref[...], mod_ref[...], 3, 4)
    _store_tiled_rows(h_ref, _pack_halves(h2))
    logits = jnp.dot(h2.astype(BF16), wr_ref[...], preferred_element_type=F32) + br_ref[...]
    lane = lax.broadcasted_iota(jnp.int32, logits.shape, 1)
    n_exp = n_groups * per_group
    is_g = lane < n_groups
    gl = jnp.where(is_g, logits, NEG_BIG)
    gmax = jnp.max(gl, axis=-1, keepdims=True)
    g_idx = jnp.min(jnp.where(gl == gmax, lane, LANES), axis=-1, keepdims=True)
    g_top = 1.0 / jnp.sum(jnp.where(is_g, jnp.exp(gl - gmax), 0.0), axis=-1, keepdims=True)
    lo = n_groups + g_idx * per_group
    in_grp = jnp.logical_and(lane >= lo, lane < lo + per_group)
    el = jnp.where(in_grp, logits, NEG_BIG)
    v1 = jnp.max(el, axis=-1, keepdims=True)
    i1 = jnp.min(jnp.where(el == v1, lane, LANES), axis=-1, keepdims=True)
    el2 = jnp.where(lane == i1, NEG_BIG, el)
    v2 = jnp.max(el2, axis=-1, keepdims=True)
    i2 = jnp.min(jnp.where(el2 == v2, lane, LANES), axis=-1, keepdims=True)
    e = jnp.exp(v2 - v1)
    w1 = g_top / (1.0 + e)
    w2 = g_top * e / (1.0 + e)
    del n_exp
    out = jnp.where(lane == 0, (i1 - n_groups).astype(F32),
                    jnp.where(lane == 1, (i2 - n_groups).astype(F32),
                              jnp.where(lane == 2, w1, jnp.where(lane == 3, w2, 0.0))))
    r_ref[...] = out


def _router(x2, g, mod_l, seq, w_router16, b_router, n_groups, per_group):
    n, d = x2.shape
    tm = _tile(seq, 256)
    per_seq = seq // tm
    return pl.pallas_call(
        functools.partial(_router_kernel, n_groups=n_groups, per_group=per_group),
        name="moe_router",
        grid=(n // tm,),
        in_specs=[
            pl.BlockSpec((tm, d), lambda i: (i, 0)),
            pl.BlockSpec((1, d), lambda i: (0, 0)),
            pl.BlockSpec((None, 6, d), lambda i: (i // per_seq, 0, 0)),
            pl.BlockSpec((d, LANES), lambda i: (0, 0)),
            pl.BlockSpec((1, LANES), lambda i: (0, 0)),
        ],
        out_specs=[
            pl.BlockSpec((d // (2 * LANES), tm, LANES), lambda i: (0, i, 0)),
            pl.BlockSpec((tm, LANES), lambda i: (i, 0)),
        ],
        out_shape=[
            jax.ShapeDtypeStruct((d // (2 * LANES), n, LANES), jnp.uint32),
            jax.ShapeDtypeStruct((n, LANES), F32),
        ],
        compiler_params=_params(("arbitrary",)),
    )(x2, g.reshape(1, d), mod_l, w_router16, b_router)


GATHER_UNROLL = 8


def _start_row_gather(idx_ref, base, n_rows, src_hbm, dst, sem):
    def body(g, c):
        r0 = pl.multiple_of(g * GATHER_UNROLL, GATHER_UNROLL)
        for k in range(GATHER_UNROLL):
            row = idx_ref[base + r0 + k]
            pltpu.make_async_copy(src_hbm.at[:, pl.ds(row, 1), :], dst.at[:, pl.ds(r0 + k, 1), :], sem).start()
        return c
    lax.fori_loop(0, n_rows // GATHER_UNROLL, body, 0)


def _wait_row_gather(src_hbm, dst, sem):
    pltpu.make_async_copy(src_hbm.at[:, pl.ds(0, dst.shape[1]), :], dst, sem).wait()


def _ffn_kernel(tile_exp_ref, n_tiles_ref, src_ref, h_hbm, w1_ref, w3_ref, w2_ref, y_ref, xbuf, sem, *, tm):
    t = pl.program_id(0)
    n_tiles = n_tiles_ref[0]
    slot = t % 2

    def start_tile(tile, s):
        _start_row_gather(src_ref, tile * tm, tm, h_hbm, xbuf.at[s], sem.at[s])

    @pl.when(jnp.logical_and(t == 0, n_tiles > 0))
    def _():
        start_tile(0, 0)

    @pl.when(t + 1 < n_tiles)
    def _():
        start_tile(t + 1, 1 - slot)

    @pl.when(t < n_tiles)
    def _():
        _wait_row_gather(h_hbm, xbuf.at[slot], sem.at[slot])
        x_lo, x_hi = (v.astype(BF16) for v in _unpack_halves(_load_tiled_rows(xbuf.at[slot])))
        half = x_lo.shape[1]

        def up_proj(w_ref):
            return (jnp.dot(x_lo, w_ref[0:half, :], preferred_element_type=F32)
                    + jnp.dot(x_hi, w_ref[half:2 * half, :], preferred_element_type=F32))

        a = up_proj(w1_ref)
        b = up_proj(w3_ref)
        hid = (a * _sigmoid(a) * b).astype(BF16)
        _store_tiled_rows(y_ref, _pack_halves(jnp.dot(hid, w2_ref[...], preferred_element_type=F32)))

    @pl.when(t >= n_tiles)
    def _():
        y_ref[...] = jnp.zeros_like(y_ref)


def _ffn(h2, tile_expert, n_tiles, src_token, w1, w3, w2, layer, tm):
    d, de = w1.shape[2], w1.shape[3]
    n_chunks = h2.shape[0]
    max_tiles = tile_expert.shape[0]
    grid_spec = pltpu.PrefetchScalarGridSpec(
        num_scalar_prefetch=3,
        grid=(max_tiles,),
        in_specs=[
            pl.BlockSpec(memory_space=pl.ANY),
            pl.BlockSpec((None, None, d, de), lambda t, te, nt, src: (layer, te[t], 0, 0)),
            pl.BlockSpec((None, None, d, de), lambda t, te, nt, src: (layer, te[t], 0, 0)),
            pl.BlockSpec((None, None, de, d), lambda t, te, nt, src: (layer, te[t], 0, 0)),
        ],
        out_specs=pl.BlockSpec((n_chunks, tm, LANES), lambda t, te, nt, src: (0, t, 0)),
        scratch_shapes=[
            pltpu.VMEM((2, n_chunks, tm, LANES), jnp.uint32),
            pltpu.SemaphoreType.DMA((2,)),
        ],
    )
    return pl.pallas_call(
        functools.partial(_ffn_kernel, tm=tm),
        name="moe_ffn",
        grid_spec=grid_spec,
        out_shape=jax.ShapeDtypeStruct((n_chunks, max_tiles * tm, LANES), jnp.uint32),
        compiler_params=_params(("arbitrary",), disable_bounds_checks=True),
    )(tile_expert, n_tiles, src_token, h2, w1, w3, w2)


def _combine_kernel(pos_ref, y_hbm, x_ref, mod_ref, w_ref, o_ref, ybuf, sem, *, tc):
    t = pl.program_id(0)
    n_steps = pl.num_programs(0)
    slot = t % 2

    def start_step(step, s):
        _start_row_gather(pos_ref, step * (TOP_K * tc), TOP_K * tc, y_hbm, ybuf.at[s], sem.at[s])

    @pl.when(t == 0)
    def _():
        start_step(0, 0)

    @pl.when(t + 1 < n_steps)
    def _():
        start_step(t + 1, 1 - slot)

    _wait_row_gather(y_hbm, ybuf.at[slot], sem.at[slot])
    half = ybuf.shape[1] * ybuf.shape[3]
    chunk = _tile(tc, 2 * SUBLANES)
    for r0 in range(0, tc, chunk):
        w = w_ref[r0:r0 + chunk, :]
        first = _unpack_halves(_load_tiled_rows(ybuf.at[slot, :, r0:r0 + chunk, :]))
        second = _unpack_halves(_load_tiled_rows(ybuf.at[slot, :, tc + r0:tc + r0 + chunk, :]))
        for part in range(2):
            cols = slice(part * half, (part + 1) * half)
            moe = w[:, 2:3] * first[part] + w[:, 3:4] * second[part]
            o_ref[r0:r0 + chunk, cols] = x_ref[r0:r0 + chunk, cols] + mod_ref[5:6, cols] * moe


def _combine(pos_steps, y_sorted, x2, mod_l, route, seq, tc):
    n, d = x2.shape
    per_seq = seq // tc
    grid_spec = pltpu.PrefetchScalarGridSpec(
        num_scalar_prefetch=1,
        grid=(n // tc,),
        in_specs=[
            pl.BlockSpec(memory_space=pl.ANY),
            pl.BlockSpec((tc, d), lambda t, pos: (t, 0)),
            pl.BlockSpec((None, 6, d), lambda t, pos: (t // per_seq, 0, 0)),
            pl.BlockSpec((tc, LANES), lambda t, pos: (t, 0)),
        ],
        out_specs=pl.BlockSpec((tc, d), lambda t, pos: (t, 0)),
        scratch_shapes=[
            pltpu.VMEM((2, y_sorted.shape[0], TOP_K * tc, LANES), jnp.uint32),
            pltpu.SemaphoreType.DMA((2,)),
        ],
    )
    return pl.pallas_call(
        functools.partial(_combine_kernel, tc=tc),
        name="moe_combine",
        grid_spec=grid_spec,
        out_shape=jax.ShapeDtypeStruct((n, d), F32),
        compiler_params=_params(("arbitrary",), disable_bounds_checks=True),
    )(pos_steps, y_sorted, x2, mod_l, route)


def _moe_plan(route, n_exp, tm, tc):
    n = route.shape[0]
    eid = route[:, 0:TOP_K].astype(jnp.int32)
    flat_e = eid.reshape(-1)
    onehot = (flat_e[:, None] == jnp.arange(n_exp, dtype=jnp.int32)[None, :]).astype(jnp.int32)
    csum = jnp.cumsum(onehot, axis=0)
    rank = jnp.sum(csum * onehot, axis=1) - 1
    counts = csum[-1]
    tiles = (counts + tm - 1) // tm
    tile_end = jnp.cumsum(tiles)
    tile_start = tile_end - tiles
    n_tiles = tile_end[-1]
    pos = jnp.sum(onehot * (tile_start * tm)[None, :], axis=1) + rank
    max_tiles = (TOP_K * n) // tm + n_exp
    token = jnp.arange(TOP_K * n, dtype=jnp.int32) // TOP_K
    src_token = jnp.zeros((max_tiles * tm,), jnp.int32).at[pos].set(token)
    tix = jnp.minimum(jnp.arange(max_tiles, dtype=jnp.int32), n_tiles - 1)
    tile_expert = jnp.sum((tile_end[None, :] <= tix[:, None]).astype(jnp.int32), axis=1)
    pos_steps = pos.reshape(n // tc, tc, TOP_K).transpose(0, 2, 1).reshape(-1)
    return tile_expert, n_tiles.reshape(1).astype(jnp.int32), src_token, pos_steps.astype(jnp.int32)


def kernel(x, c, positions, ada_w, ada_b, norm1_g, norm2_g, w_in, b_gate, lru_conv_w, lru_conv_b, lru_wr, lru_br, lru_wi, lru_bi, lru_lambda, pool_w, pool_b, pool_scale, q_norm_g, k_norm_g, lam_q1, lam_k1, lam_q2, lam_k2, subln_g, cv_dw_w, cv_dw_b, cv_ln_g, cv_ln_b, w_branch, w_out, router_g_w, router_g_b, router_e_w, router_e_b, moe_w1, moe_w3, moe_w2):
    bsz, seq, d = x.shape
    depth = ada_w.shape[0]
    db = lru_conv_w.shape[2]
    n_branch = w_branch.shape[1]
    n_mix = w_in.shape[2] - n_branch * d
    assert n_mix == 8 * db and q_norm_g.shape[1] == HEAD_DIM
    n_groups = router_g_w.shape[2]
    n_exp = router_e_w.shape[2]
    per_group = n_exp // n_groups
    assert n_groups + n_exp <= LANES
    n = bsz * seq
    col = {name: idx for idx, name in enumerate(("lru_x", "lru_gate", "pool", "q", "k", "v", "glu_val", "glu_gate"))}

    mod = _ada_mod(c, ada_w, ada_b).reshape(depth, bsz, 6, d)
    cos_t, sa_t, sb_t = _rope_tables(positions)
    seg = jnp.kron(jnp.eye(HEAD_W // HEAD_DIM, dtype=F32), jnp.full((HEAD_DIM, HEAD_DIM), 1.0 / HEAD_DIM, F32)).astype(BF16)
    dup = lambda g: jnp.concatenate([g, g]).reshape(1, HEAD_W)
    ffn_tm = _tile(TOP_K * n, 256)
    comb_tc = _tile(seq, 256)

    w_in16, w_branch16, w_out16 = w_in.astype(BF16), w_branch.astype(BF16), w_out.astype(BF16)
    moe_w1_16, moe_w3_16, moe_w2_16 = moe_w1.astype(BF16), moe_w3.astype(BF16), moe_w2.astype(BF16)

    x2 = x.reshape(n, d)
    for l in range(depth):
        mod_l = mod[l]
        h = _norm(x2, norm1_g[l], mod_l, seq, 0, 1)
        proj = _matmul(h, w_in16, l, n_mix)
        y_a = _lru_branch(proj, bsz, seq, col["lru_x"], col["lru_gate"], lru_conv_w[l], lru_conv_b[l],
                          lru_wr[l], lru_br[l], lru_wi[l], lru_bi[l], lru_lambda[l])
        y_b = _pool_branch(proj, bsz, seq, col["pool"], pool_w[l], pool_b[l], pool_scale[l])
        qn, kt = _qk_prep(proj, bsz, seq, col["q"], col["k"], db, cos_t, sa_t, sb_t, seg,
                          dup(q_norm_g[l]), dup(k_norm_g[l]))
        lambda_init = 0.8 - 0.6 * math.exp(-0.3 * l)
        y_c = _flash(qn, kt, proj, bsz, seq, col["v"], lam_q1[l], lam_k1[l], lam_q2[l], lam_k2[l],
                     subln_g[l], lambda_init)
        y_d = _conv_branch(proj, bsz, seq, col["glu_val"], col["glu_gate"], cv_dw_w[l], cv_dw_b[l],
                           cv_ln_g[l], cv_ln_b[l])
        merged = _merge(h, (y_a, y_b, y_c, y_d), w_in16, l, n_mix, b_gate[l], w_branch16)
        x2 = _resid_matmul(merged, w_out16, l, x2, mod_l, seq, 2)

        w_router = jnp.concatenate([router_g_w[l], router_e_w[l]], axis=1)
        w_router = jnp.pad(w_router, ((0, 0), (0, LANES - w_router.shape[1]))).astype(BF16)
        b_router = jnp.pad(jnp.concatenate([router_g_b[l], router_e_b[l]]), (0, LANES - n_groups - n_exp)).reshape(1, LANES)
        h2, route = _router(x2, norm2_g[l], mod_l, seq, w_router, b_router, n_groups, per_group)
        tile_expert, n_tiles, src_token, pos_steps = _moe_plan(route, n_exp, ffn_tm, comb_tc)
        y_sorted = _ffn(h2, tile_expert, n_tiles, src_token, moe_w1_16, moe_w3_16, moe_w2_16, l, ffn_tm)
        x2 = _combine(pos_steps, y_sorted, x2, mod_l, route, seq, comb_tc)
    return x2.reshape(bsz, seq, d)
```

```python
import functools
import math

import jax
import jax.numpy as jnp
from jax import lax
from jax.experimental import pallas as pl
from jax.experimental.pallas import tpu as pltpu

EPS = 1e-6
LRU_C = 8.0
LRU_CONV_TAPS = 4
POOL_WINDOWS = (2, 4, 8, 16)
POOL_HALO = 16
CONV_HALO = 32
HEAD_DIM = 64
HEAD_W = 2 * HEAD_DIM
ROPE_DIM = HEAD_DIM // 4
ROPE_HALF = ROPE_DIM // 2
ROPE_THETA = 500000.0
TOP_K = 2
NEG_BIG = -1e30
LOG2_E = math.log2(math.e)
SUBLANES = 8
LANES = 128
VMEM_LIMIT_BYTES = 56 * 1024 * 1024

BF16 = jnp.bfloat16
F32 = jnp.float32


def _tile(n, pref):
    t = min(n, pref)
    assert n % t == 0, (n, pref)
    return t


def _params(sem, **kw):
    return pltpu.CompilerParams(dimension_semantics=sem, vmem_limit_bytes=VMEM_LIMIT_BYTES, **kw)


def _sigmoid(x):
    return 1.0 / (1.0 + jnp.exp(-x))


def _ada_kernel(c_ref, w_ref, b_ref, o_ref):
    c = c_ref[...]
    c_act = (c * _sigmoid(c)).astype(BF16)
    o_ref[...] = jnp.dot(c_act, w_ref[...].astype(BF16), preferred_element_type=F32) + b_ref[...]


def _ada_mod(c, ada_w, ada_b):
    depth, d, n_out = ada_w.shape
    bsz = c.shape[0]
    tn = _tile(n_out, 1024)
    return pl.pallas_call(
        _ada_kernel,
        name="ada_mod",
        grid=(depth, n_out // tn),
        in_specs=[
            pl.BlockSpec((bsz, d), lambda l, j: (0, 0)),
            pl.BlockSpec((None, d, tn), lambda l, j: (l, 0, j)),
            pl.BlockSpec((None, 1, tn), lambda l, j: (l, 0, j)),
        ],
        out_specs=pl.BlockSpec((None, bsz, tn), lambda l, j: (l, 0, j)),
        out_shape=jax.ShapeDtypeStruct((depth, bsz, n_out), F32),
        compiler_params=_params(("arbitrary", "arbitrary")),
    )(c, ada_w, ada_b.reshape(depth, 1, n_out))


def _modnorm(x, g, mod, shift_row, scale_row):
    y = x * lax.rsqrt(jnp.mean(x * x, axis=-1, keepdims=True) + EPS) * g
    return y * (1.0 + mod[scale_row:scale_row + 1, :]) + mod[shift_row:shift_row + 1, :]


def _norm_kernel(x_ref, g_ref, mod_ref, o_ref, *, shift_row, scale_row):
    o_ref[...] = _modnorm(x_ref[...], g_ref[...], mod_ref[...], shift_row, scale_row).astype(o_ref.dtype)


def _norm(x2, g, mod_l, seq, shift_row, scale_row):
    n, d = x2.shape
    tm = _tile(seq, 512)
    per_seq = seq // tm
    return pl.pallas_call(
        functools.partial(_norm_kernel, shift_row=shift_row, scale_row=scale_row),
        name="mod_norm",
        grid=(n // tm,),
        in_specs=[
            pl.BlockSpec((tm, d), lambda i: (i, 0)),
            pl.BlockSpec((1, d), lambda i: (0, 0)),
            pl.BlockSpec((None, 6, d), lambda i: (i // per_seq, 0, 0)),
        ],
        out_specs=pl.BlockSpec((tm, d), lambda i: (i, 0)),
        out_shape=jax.ShapeDtypeStruct((n, d), BF16),
        compiler_params=_params(("arbitrary",)),
    )(x2, g.reshape(1, d), mod_l)


def _matmul_kernel(a_ref, w_ref, o_ref):
    o_ref[...] = jnp.dot(a_ref[...], w_ref[...], preferred_element_type=F32).astype(o_ref.dtype)


def _matmul(a, w, layer, n_cols):
    m, k = a.shape
    tm = _tile(m, 1024)
    tn = _tile(n_cols, 1024)
    return pl.pallas_call(
        _matmul_kernel,
        name="in_proj",
        grid=(m // tm, n_cols // tn),
        in_specs=[
            pl.BlockSpec((tm, k), lambda i, j: (i, 0)),
            pl.BlockSpec((None, k, tn), lambda i, j: (layer, 0, j)),
        ],
        out_specs=pl.BlockSpec((tm, tn), lambda i, j: (i, j)),
        out_shape=jax.ShapeDtypeStruct((m, n_cols), BF16),
        compiler_params=_params(("arbitrary", "arbitrary")),
    )(a, w)


def _gelu_tanh(x):
    return 0.5 * x * (1.0 + jnp.tanh(math.sqrt(2.0 / math.pi) * (x + 0.044715 * (x * x * x))))


def _lru_kernel(x_ref, gate_ref, cw_ref, cb_ref, wr_ref, br_ref, wi_ref, bi_ref, lam_ref, o_ref,
                xe_ref, a_ref, u_ref, carry_ref):
    ts, db = x_ref.shape
    n_blocks, blk = wr_ref.shape[0], wr_ref.shape[1]
    halo = SUBLANES
    first = pl.program_id(1) == 0

    @pl.when(first)
    def _():
        xe_ref[0:halo, :] = jnp.zeros((halo, db), F32)
        carry_ref[...] = jnp.zeros_like(carry_ref)

    @pl.when(jnp.logical_not(first))
    def _():
        xe_ref[0:halo, :] = xe_ref[ts:ts + halo, :]

    xe_ref[halo:halo + ts, :] = x_ref[...].astype(F32)

    xa = jnp.zeros((ts, db), F32) + cb_ref[...]
    for j in range(LRU_CONV_TAPS):
        off = halo - (LRU_CONV_TAPS - 1) + j
        xa = xa + cw_ref[j:j + 1, :] * xe_ref[off:off + ts, :]

    xa16 = xa.astype(BF16)
    for hb in range(n_blocks):
        sl = slice(hb * blk, (hb + 1) * blk)
        a_ref[:, sl] = jnp.dot(xa16[:, sl], wr_ref[hb], preferred_element_type=F32)
        u_ref[:, sl] = jnp.dot(xa16[:, sl], wi_ref[hb], preferred_element_type=F32)
    r = _sigmoid(a_ref[...] + br_ref[...])
    ig = _sigmoid(u_ref[...] + bi_ref[...])
    z = -lam_ref[...]
    softplus = jnp.maximum(z, 0.0) + jnp.log(1.0 + jnp.exp(-jnp.abs(z)))
    a = jnp.exp((-LRU_C) * r * softplus)
    u = jnp.sqrt(1.0 - a * a) * (ig * xa)

    row = lax.broadcasted_iota(jnp.int32, (ts, db), 0) % SUBLANES
    for d in (1, 2, 4):
        keep = row >= d
        a_sh = jnp.where(keep, pltpu.roll(a, d, 0), 1.0)
        u_sh = jnp.where(keep, pltpu.roll(u, d, 0), 0.0)
        u = a * u_sh + u
        a = a * a_sh
    a_ref[...] = a
    u_ref[...] = u

    def body(c, h_prev):
        r0 = pl.multiple_of(c * SUBLANES, SUBLANES)
        h8 = u_ref[pl.ds(r0, SUBLANES), :] + a_ref[pl.ds(r0, SUBLANES), :] * h_prev
        u_ref[pl.ds(r0, SUBLANES), :] = h8
        return h8[SUBLANES - 1:SUBLANES, :]

    carry_ref[...] = lax.fori_loop(0, ts // SUBLANES, body, carry_ref[...])
    o_ref[...] = (_gelu_tanh(gate_ref[...].astype(F32)) * u_ref[...]).astype(o_ref.dtype)


def _lru_branch(proj, bsz, seq, col_x, col_gate, cw, cb, wr, br, wi, bi, lam):
    db = cw.shape[1]
    ts = _tile(seq, 256)
    per_seq = seq // ts
    n_blocks, blk = wr.shape[0], wr.shape[1]
    row = lambda v: v.reshape(1, db)
    vec = pl.BlockSpec((1, db), lambda b, i: (0, 0))
    wspec = pl.BlockSpec((n_blocks, blk, blk), lambda b, i: (0, 0, 0))
    return pl.pallas_call(
        _lru_kernel,
        name="lru_branch",
        grid=(bsz, per_seq),
        in_specs=[
            pl.BlockSpec((ts, db), lambda b, i: (b * per_seq + i, col_x)),
            pl.BlockSpec((ts, db), lambda b, i: (b * per_seq + i, col_gate)),
            pl.BlockSpec((LRU_CONV_TAPS, db), lambda b, i: (0, 0)),
            vec, wspec, vec, wspec, vec, vec,
        ],
        out_specs=pl.BlockSpec((ts, db), lambda b, i: (b * per_seq + i, 0)),
        out_shape=jax.ShapeDtypeStruct((bsz * seq, db), BF16),
        scratch_shapes=[
            pltpu.VMEM((ts + SUBLANES, db), F32),
            pltpu.VMEM((ts, db), F32),
            pltpu.VMEM((ts, db), F32),
            pltpu.VMEM((1, db), F32),
        ],
        compiler_params=_params(("arbitrary", "arbitrary")),
    )(proj, proj, cw, row(cb), wr.astype(BF16), row(br), wi.astype(BF16), row(bi), row(lam))


def _pool_kernel(x_ref, w_ref, b_ref, s_ref, o_ref, xe_ref):
    ts, db = x_ref.shape
    n_groups, pg = w_ref.shape[0], w_ref.shape[1]
    halo = POOL_HALO
    i = pl.program_id(1)

    @pl.when(i == 0)
    def _():
        xe_ref[0:halo, :] = jnp.zeros((halo, db), F32)

    @pl.when(i != 0)
    def _():
        xe_ref[0:halo, :] = xe_ref[ts:ts + halo, :]

    xe_ref[halo:halo + ts, :] = x_ref[...].astype(F32)
    t1 = (i * ts + 1 + lax.broadcasted_iota(jnp.int32, (ts, pg), 0)).astype(F32)
    for gi in range(n_groups):
        win = POOL_WINDOWS[gi]
        sl = slice(gi * pg, (gi + 1) * pg)
        x = xe_ref[halo:halo + ts, sl]
        acc = x
        for j in range(1, win):
            acc = acc + xe_ref[halo - j:halo - j + ts, sl]
        diff = acc / jnp.minimum(t1, float(win)) - x
        y = jnp.dot(diff.astype(BF16), w_ref[gi], preferred_element_type=F32) + b_ref[:, sl]
        o_ref[:, sl] = (y * s_ref[:, sl]).astype(o_ref.dtype)


def _pool_branch(proj, bsz, seq, col, w, b, scale):
    n_groups, pg = w.shape[0], w.shape[1]
    assert n_groups == len(POOL_WINDOWS)
    db = n_groups * pg
    ts = _tile(seq, 512)
    per_seq = seq // ts
    vec = pl.BlockSpec((1, db), lambda bb, i: (0, 0))
    return pl.pallas_call(
        _pool_kernel,
        name="pool_branch",
        grid=(bsz, per_seq),
        in_specs=[
            pl.BlockSpec((ts, db), lambda bb, i: (bb * per_seq + i, col)),
            pl.BlockSpec((n_groups, pg, pg), lambda bb, i: (0, 0, 0)),
            vec, vec,
        ],
        out_specs=pl.BlockSpec((ts, db), lambda bb, i: (bb * per_seq + i, 0)),
        out_shape=jax.ShapeDtypeStruct((bsz * seq, db), BF16),
        scratch_shapes=[pltpu.VMEM((ts + POOL_HALO, db), F32)],
        compiler_params=_params(("arbitrary", "arbitrary")),
    )(proj, w.astype(BF16), b.reshape(1, db), scale.reshape(1, db))


def _conv_kernel(val_ref, gate_ref, w_ref, b_ref, g_ref, beta_ref, o_ref, ue_ref, sh_ref, *, row_block):
    ts, db = val_ref.shape
    taps = w_ref.shape[0]
    halo = CONV_HALO
    i = pl.program_id(1)

    @pl.when(i == 0)
    def _():
        ue_ref[0:halo, :] = jnp.zeros((halo, db), F32)

    @pl.when(i != 0)
    def _():
        ue_ref[0:halo, :] = ue_ref[ts:ts + halo, :]

    ue_ref[halo:halo + ts, :] = val_ref[...].astype(F32) * _sigmoid(gate_ref[...].astype(F32))
    span = sh_ref.shape[1]
    for p in range(1, SUBLANES):
        sh_ref[p - 1] = ue_ref[p:p + span, :]
    base = halo - (taps - 1)
    for rb in range(ts // row_block):
        r0 = rb * row_block
        acc = jnp.zeros((row_block, db), F32) + b_ref[...]
        for j in range(taps):
            a, p = divmod(base + j, SUBLANES)
            row = r0 + a * SUBLANES
            window = ue_ref[row:row + row_block, :] if p == 0 else sh_ref[p - 1, row:row + row_block, :]
            acc = acc + w_ref[j:j + 1, :] * window
        mu = jnp.mean(acc, axis=-1, keepdims=True)
        cen = acc - mu
        var = jnp.mean(cen * cen, axis=-1, keepdims=True)
        y = cen * lax.rsqrt(var + EPS) * g_ref[...] + beta_ref[...]
        o_ref[r0:r0 + row_block, :] = (y * _sigmoid(y)).astype(o_ref.dtype)


def _conv_branch(proj, bsz, seq, col_val, col_gate, w, b, g, beta):
    taps, db = w.shape
    assert taps - 1 <= CONV_HALO
    ts = _tile(seq, 256)
    per_seq = seq // ts
    row = lambda v: v.reshape(1, db)
    vec = pl.BlockSpec((1, db), lambda bb, i: (0, 0))
    return pl.pallas_call(
        functools.partial(_conv_kernel, row_block=_tile(ts, 32)),
        name="conv_branch",
        grid=(bsz, per_seq),
        in_specs=[
            pl.BlockSpec((ts, db), lambda bb, i: (bb * per_seq + i, col_val)),
            pl.BlockSpec((ts, db), lambda bb, i: (bb * per_seq + i, col_gate)),
            pl.BlockSpec((taps, db), lambda bb, i: (0, 0)),
            vec, vec, vec,
        ],
        out_specs=pl.BlockSpec((ts, db), lambda bb, i: (bb * per_seq + i, 0)),
        out_shape=jax.ShapeDtypeStruct((bsz * seq, db), BF16),
        scratch_shapes=[
            pltpu.VMEM((ts + CONV_HALO, db), F32),
            pltpu.VMEM((SUBLANES - 1, ts + CONV_HALO - SUBLANES, db), F32),
        ],
        compiler_params=_params(("arbitrary", "arbitrary")),
    )(proj, proj, w, row(b), row(g), row(beta))


def _qk_prep_kernel(q_ref, k_ref, cos_ref, sa_ref, sb_ref, seg_ref, qg_ref, kg_ref, qo_ref, kt_ref, kn_ref):
    ts, db = q_ref.shape
    cos, sa, sb = cos_ref[...], sa_ref[...], sb_ref[...]

    def norm_rope(x, g):
        ms = jnp.dot((x * x).astype(BF16), seg_ref[...], preferred_element_type=F32)
        xn = x * lax.rsqrt(ms + EPS) * g
        return (xn * cos + pltpu.roll(xn, HEAD_W - ROPE_HALF, 1) * sa + pltpu.roll(xn, ROPE_HALF, 1) * sb)

    for h in range(db // HEAD_W):
        sl = slice(h * HEAD_W, (h + 1) * HEAD_W)
        q = norm_rope(q_ref[:, sl].astype(F32), qg_ref[...])
        qo_ref[:, sl] = (q * (LOG2_E / math.sqrt(HEAD_DIM))).astype(qo_ref.dtype)
        kn_ref[:, sl] = norm_rope(k_ref[:, sl].astype(F32), kg_ref[...])
    kt_ref[...] = kn_ref[...].T.astype(kt_ref.dtype)


def _qk_prep(proj, bsz, seq, col_q, col_k, db, cos_t, sa_t, sb_t, seg, qg, kg):
    ts = _tile(seq, 256)
    per_seq = seq // ts
    tab = pl.BlockSpec((ts, HEAD_W), lambda b, i: (b * per_seq + i, 0))
    vec = pl.BlockSpec((1, HEAD_W), lambda b, i: (0, 0))
    return pl.pallas_call(
        _qk_prep_kernel,
        name="qk_prep",
        grid=(bsz, per_seq),
        in_specs=[
            pl.BlockSpec((ts, db), lambda b, i: (b * per_seq + i, col_q)),
            pl.BlockSpec((ts, db), lambda b, i: (b * per_seq + i, col_k)),
            tab, tab, tab,
            pl.BlockSpec((HEAD_W, HEAD_W), lambda b, i: (0, 0)),
            vec, vec,
        ],
        out_specs=[
            pl.BlockSpec((ts, db), lambda b, i: (b * per_seq + i, 0)),
            pl.BlockSpec((None, db, ts), lambda b, i: (b, 0, i)),
        ],
        out_shape=[
            jax.ShapeDtypeStruct((bsz * seq, db), BF16),
            jax.ShapeDtypeStruct((bsz, db, seq), BF16),
        ],
        scratch_shapes=[pltpu.VMEM((ts, db), F32)],
        compiler_params=_params(("arbitrary", "arbitrary")),
    )(proj, proj, cos_t, sa_t, sb_t, seg, qg, kg)


def _flash_kernel(q_ref, kt_ref, v_ref, lq1_ref, lk1_ref, lq2_ref, lk2_ref, sg_ref, o_ref,
                  m_ref, l_ref, acc_ref, p_ref, *, lambda_init, tk):
    tq = q_ref.shape[0]
    rows = 2 * tq
    i = pl.program_id(2)
    lane = lax.broadcasted_iota(jnp.int32, (tq, HEAD_W), 1)
    q = q_ref[...]
    zero = jnp.zeros_like(q)
    q2 = jnp.concatenate([jnp.where(lane < HEAD_DIM, q, zero), jnp.where(lane >= HEAD_DIM, q, zero)], axis=0)
    m_ref[...] = jnp.full(m_ref.shape, NEG_BIG, F32)
    l_ref[...] = jnp.zeros(l_ref.shape, F32)
    acc_ref[...] = jnp.zeros(acc_ref.shape, F32)
    p_ref[...] = jnp.zeros(p_ref.shape, p_ref.dtype)

    def pv(j):
        k0 = pl.multiple_of(j * tk, tk)
        return jnp.dot(p_ref[...], v_ref[pl.ds(k0, tk), :], preferred_element_type=F32)

    n_full = (i * tq) // tk
    n_all = ((i + 1) * tq + tk - 1) // tk

    def step(j, masked):
        k0 = pl.multiple_of(j * tk, tk)
        s = jnp.dot(q2, kt_ref[:, pl.ds(k0, tk)], preferred_element_type=F32)
        pv_prev = pv(jnp.maximum(j - 1, 0))
        if masked:
            qpos = i * tq + lax.broadcasted_iota(jnp.int32, (rows, tk), 0) % tq
            kpos = k0 + lax.broadcasted_iota(jnp.int32, (rows, tk), 1)
            s = jnp.where(kpos <= qpos, s, NEG_BIG)
        cols = [s[:, cc * LANES:(cc + 1) * LANES] for cc in range(tk // LANES)]
        m_prev = m_ref[...]
        m_next = jnp.maximum(m_prev, jnp.max(functools.reduce(jnp.maximum, cols), axis=-1, keepdims=True))
        alpha = jnp.exp2(m_prev - m_next)
        ps = [jnp.exp2(col - m_next) for col in cols]
        l_ref[...] = alpha * l_ref[...] + functools.reduce(jnp.add, ps)
        acc_ref[...] = alpha * (acc_ref[...] + pv_prev)
        p_ref[...] = jnp.concatenate([x.astype(p_ref.dtype) for x in ps], axis=1)
        m_ref[...] = m_next

    def full_body(j, c):
        step(j, False)
        return c

    def diag_body(j, c):
        step(j, True)
        return c

    lax.fori_loop(0, n_full, full_body, 0)
    lax.fori_loop(n_full, n_all, diag_body, 0)

    lam = (jnp.exp(jnp.sum(lq1_ref[...] * lk1_ref[...], axis=-1, keepdims=True))
           - jnp.exp(jnp.sum(lq2_ref[...] * lk2_ref[...], axis=-1, keepdims=True)) + lambda_init)
    o_all = (acc_ref[...] + pv(n_all - 1)) / jnp.sum(l_ref[...], axis=-1, keepdims=True)
    o = o_all[0:tq, :] - lam * o_all[tq:rows, :]
    o = o * lax.rsqrt(jnp.mean(o * o, axis=-1, keepdims=True) + EPS) * sg_ref[...]
    o_ref[...] = (o * (1.0 - lambda_init)).astype(o_ref.dtype)


def _flash(qn, kt, proj, bsz, seq, col_v, lq1, lk1, lq2, lk2, sub_g, lambda_init):
    db = qn.shape[1]
    heads = db // HEAD_W
    tq = _tile(seq, 512)
    tk = _tile(seq, 512)
    per_seq = seq // tq
    vec64 =pl.BlockSpec((1, HEAD_DIM), lambda b, h, i: (0, 0))
    row64 = lambda v: v.reshape(1, HEAD_DIM)
    return pl.pallas_call(
        functools.partial(_flash_kernel, lambda_init=lambda_init, tk=tk),
        name="diff_attn",
        grid=(bsz, heads, per_seq),
        in_specs=[
            pl.BlockSpec((tq, HEAD_W), lambda b, h, i: (b * per_seq + i, h)),
            pl.BlockSpec((None, HEAD_W, seq), lambda b, h, i: (b, h, 0)),
            pl.BlockSpec((seq, HEAD_W), lambda b, h, i: (b, col_v * heads + h)),
            vec64, vec64, vec64, vec64,
            pl.BlockSpec((1, HEAD_W), lambda b, h, i: (0, 0)),
        ],
        out_specs=pl.BlockSpec((tq, HEAD_W), lambda b, h, i: (b * per_seq + i, h)),
        out_shape=jax.ShapeDtypeStruct((bsz * seq, db), BF16),
        scratch_shapes=[pltpu.VMEM((2 * tq, HEAD_W), F32)] * 3 + [pltpu.VMEM((2 * tq, tk), BF16)],
        compiler_params=_params(("arbitrary", "arbitrary", "arbitrary")),
    )(qn, kt, proj, row64(lq1), row64(lk1), row64(lq2), row64(lk2), sub_g.reshape(1, HEAD_W))


def _rope_tables(positions):
    inv_freq = ROPE_THETA ** (-jnp.arange(0, ROPE_DIM, 2, dtype=F32) / ROPE_DIM)
    ang = positions.astype(F32).reshape(-1, 1) * inv_freq
    cos, sin = jnp.cos(ang), jnp.sin(ang)
    n = ang.shape[0]
    rest = HEAD_DIM - ROPE_DIM
    zeros_h = jnp.zeros((n, ROPE_HALF), F32)
    cos_s = jnp.concatenate([cos, cos, jnp.ones((n, rest), F32)], axis=1)
    sa_s = jnp.concatenate([-sin, zeros_h, jnp.zeros((n, rest), F32)], axis=1)
    sb_s = jnp.concatenate([zeros_h, sin, jnp.zeros((n, rest), F32)], axis=1)
    rep = lambda t: jnp.concatenate([t, t], axis=1)
    return rep(cos_s), rep(sa_s), rep(sb_s)


def _merge_kernel(h_ref, ya_ref, yb_ref, yc_ref, yd_ref, wa_ref, wb_ref, wc_ref, wd_ref,
                  bg_ref, wbr_ref, o_ref):
    h = h_ref[...]
    acc = None
    for bi, (y_ref, wg_ref) in enumerate(((ya_ref, wa_ref), (yb_ref, wb_ref), (yc_ref, wc_ref), (yd_ref, wd_ref))):
        gate = _sigmoid(jnp.dot(h, wg_ref[...], preferred_element_type=F32) + bg_ref[bi:bi + 1, :])
        term = gate * jnp.dot(y_ref[...], wbr_ref[bi], preferred_element_type=F32)
        acc = term if acc is None else acc + term
    o_ref[...] = acc.astype(o_ref.dtype)


def _merge(h, ys, w_in16, layer, n_mix, b_gate, w_branch16):
    n, d = h.shape
    db = ys[0].shape[1]
    n_branch = len(ys)
    tm = _tile(n, 1024)
    tn = _tile(d, 256)
    gate_blk = [(n_mix + b * d) // tn for b in range(n_branch)]
    once = pl.Buffered(1)
    wspecs = [pl.BlockSpec((None, d, tn), functools.partial(lambda j, i, base: (layer, 0, base + j), base=gb),
                           pipeline_mode=once) for gb in gate_blk]
    yspec = pl.BlockSpec((tm, db), lambda j, i: (i, 0))
    return pl.pallas_call(
        _merge_kernel,
        name="gated_merge",
        grid=(d // tn, n // tm),
        in_specs=[pl.BlockSpec((tm, d), lambda j, i: (i, 0))] + [yspec] * n_branch + wspecs + [
            pl.BlockSpec((n_branch, tn), lambda j, i: (0, j)),
            pl.BlockSpec((None, n_branch, db, tn), lambda j, i: (layer, 0, 0, j), pipeline_mode=once),
        ],
        out_specs=pl.BlockSpec((tm, tn), lambda j, i: (i, j)),
        out_shape=jax.ShapeDtypeStruct((n, d), BF16),
        compiler_params=_params(("arbitrary", "arbitrary")),
    )(h, *ys, *([w_in16] * n_branch), b_gate, w_branch16)


def _resid_matmul_kernel(a_ref, w_ref, x_ref, mod_ref, o_ref, *, gate_row):
    y = jnp.dot(a_ref[...], w_ref[...], preferred_element_type=F32)
    o_ref[...] = x_ref[...] + mod_ref[gate_row:gate_row + 1, :] * y


def _resid_matmul(a, w16, layer, x2, mod_l, seq, gate_row):
    n, k = a.shape
    d = w16.shape[2]
    tm = _tile(seq, 1024)
    tn = _tile(d, 1024)
    per_seq = seq // tm
    return pl.pallas_call(
        functools.partial(_resid_matmul_kernel, gate_row=gate_row),
        name="out_proj_resid",
        grid=(n // tm, d // tn),
        in_specs=[
            pl.BlockSpec((tm, k), lambda i, j: (i, 0)),
            pl.BlockSpec((None, k, tn), lambda i, j: (layer, 0, j)),
            pl.BlockSpec((tm, tn), lambda i, j: (i, j)),
            pl.BlockSpec((None, 6, tn), lambda i, j: (i // per_seq, 0, j)),
        ],
        out_specs=pl.BlockSpec((tm, tn), lambda i, j: (i, j)),
        out_shape=jax.ShapeDtypeStruct((n, d), F32),
        compiler_params=_params(("arbitrary", "arbitrary")),
    )(a, w16, x2, mod_l)


def _pack_halves(v):
    w = v.shape[1] // 2
    bits = lambda t: pltpu.bitcast(t.astype(BF16).astype(F32), jnp.uint32)
    return bits(v[:, w:]) | (bits(v[:, :w]) >> 16)


def _unpack_halves(p):
    return (pltpu.bitcast(p << 16, F32), pltpu.bitcast(p & jnp.uint32(0xFFFF0000), F32))


def _store_tiled_rows(ref3, val2):
    for s in range(ref3.shape[0]):
        ref3[s] = val2[:, s * LANES:(s + 1) * LANES]


def _load_tiled_rows(ref3):
    return jnp.concatenate([ref3[s] for s in range(ref3.shape[0])], axis=1)


def _router_kernel(x_ref, g_ref, mod_ref, wr_ref, br_ref, h_ref, r_ref, *, n_groups, per_group):
    h2 = _modnorm(x_ref[...], g_ref[...], mod_ref[...], 3, 4)
    _store_tiled_rows(h_ref, _pack_halves(h2))
    logits = jnp.dot(h2.astype(BF16), wr_ref[...], preferred_element_type=F32) + br_ref[...]
    lane = lax.broadcasted_iota(jnp.int32, logits.shape, 1)
    n_exp = n_groups * per_group
    is_g = lane < n_groups
    gl = jnp.where(is_g, logits, NEG_BIG)
    gmax = jnp.max(gl, axis=-1, keepdims=True)
    g_idx = jnp.min(jnp.where(gl == gmax, lane, LANES), axis=-1, keepdims=True)
    g_top = 1.0 / jnp.sum(jnp.where(is_g, jnp.exp(gl - gmax), 0.0), axis=-1, keepdims=True)
    lo = n_groups + g_idx * per_group
    in_grp = jnp.logical_and(lane >= lo, lane < lo + per_group)
    el = jnp.where(in_grp, logits, NEG_BIG)
    v1 = jnp.max(el, axis=-1, keepdims=True)
    i1 = jnp.min(jnp.where(el == v1, lane, LANES), axis=-1, keepdims=True)
    el2 = jnp.where(lane == i1, NEG_BIG, el)
    v2 = jnp.max(el2, axis=-1, keepdims=True)
    i2 = jnp.min(jnp.where(el2 == v2, lane, LANES), axis=-1, keepdims=True)
    e = jnp.exp(v2 - v1)
    w1 = g_top / (1.0 + e)
    w2 = g_top * e / (1.0 + e)
    del n_exp
    out = jnp.where(lane == 0, (i1 - n_groups).astype(F32),
                    jnp.where(lane == 1, (i2 - n_groups).astype(F32),
                              jnp.where(lane == 2, w1, jnp.where(lane == 3, w2, 0.0))))
    r_ref[...] = out


def _router(x2, g, mod_l, seq, w_router16, b_router, n_groups, per_group):
    n, d = x2.shape
    tm = _tile(seq, 256)
    per_seq = seq // tm
    return pl.pallas_call(
        functools.partial(_router_kernel, n_groups=n_groups, per_group=per_group),
        name="moe_router",
        grid=(n // tm,),
        in_specs=[
            pl.BlockSpec((tm, d), lambda i: (i, 0)),
            pl.BlockSpec((1, d), lambda i: (0, 0)),
            pl.BlockSpec((None, 6, d), lambda i: (i // per_seq, 0, 0)),
            pl.BlockSpec((d, LANES), lambda i: (0, 0)),
            pl.BlockSpec((1, LANES), lambda i: (0, 0)),
        ],
        out_specs=[
            pl.BlockSpec((d // (2 * LANES), tm, LANES), lambda i: (0, i, 0)),
            pl.BlockSpec((tm, LANES), lambda i: (i, 0)),
        ],
        out_shape=[
            jax.ShapeDtypeStruct((d // (2 * LANES), n, LANES), jnp.uint32),
            jax.ShapeDtypeStruct((n, LANES), F32),
        ],
        compiler_params=_params(("arbitrary",)),
    )(x2, g.reshape(1, d), mod_l, w_router16, b_router)


GATHER_UNROLL = 8


def _start_row_gather(idx_ref, base, n_rows, src_hbm, dst, sem):
    def body(g, c):
        r0 = pl.multiple_of(g * GATHER_UNROLL, GATHER_UNROLL)
        for k in range(GATHER_UNROLL):
            row = idx_ref[base + r0 + k]
            pltpu.make_async_copy(src_hbm.at[:, pl.ds(row, 1), :], dst.at[:, pl.ds(r0 + k, 1), :],
                                  sem).start(priority=k % 2)
        return c
    lax.fori_loop(0, n_rows // GATHER_UNROLL, body, 0)


def _wait_row_gather(src_hbm, dst, sem):
    pltpu.make_async_copy(src_hbm.at[:, pl.ds(0, dst.shape[1]), :], dst, sem).wait()


def _ffn_kernel(tile_exp_ref, n_tiles_ref, src_ref, h_hbm, w1_ref, w3_ref, w2_ref, y_ref, xbuf, sem, *, tm):
    t = pl.program_id(0)
    n_tiles = n_tiles_ref[0]
    slot = t % 2

    def start_tile(tile, s):
        _start_row_gather(src_ref, tile * tm, tm, h_hbm, xbuf.at[s], sem.at[s])

    @pl.when(jnp.logical_and(t == 0, n_tiles > 0))
    def _():
        start_tile(0, 0)

    @pl.when(t + 1 < n_tiles)
    def _():
        start_tile(t + 1, 1 - slot)

    @pl.when(t < n_tiles)
    def _():
        _wait_row_gather(h_hbm, xbuf.at[slot], sem.at[slot])
        x_lo, x_hi = (v.astype(BF16) for v in _unpack_halves(_load_tiled_rows(xbuf.at[slot])))
        half = x_lo.shape[1]

        def up_proj(w_ref):
            return (jnp.dot(x_lo, w_ref[0:half, :], preferred_element_type=F32)
                    + jnp.dot(x_hi, w_ref[half:2 * half, :], preferred_element_type=F32))

        a = up_proj(w1_ref)
        b = up_proj(w3_ref)
        hid = (a * _sigmoid(a) * b).astype(BF16)
        _store_tiled_rows(y_ref, _pack_halves(jnp.dot(hid, w2_ref[...], preferred_element_type=F32)))

    @pl.when(t >= n_tiles)
    def _():
        y_ref[...] = jnp.zeros_like(y_ref)


def _ffn(h2, tile_expert, n_tiles, src_token, w1, w3, w2, layer, tm):
    d, de = w1.shape[2], w1.shape[3]
    n_chunks = h2.shape[0]
    max_tiles = tile_expert.shape[0]
    grid_spec = pltpu.PrefetchScalarGridSpec(
        num_scalar_prefetch=3,
        grid=(max_tiles,),
        in_specs=[
            pl.BlockSpec(memory_space=pl.ANY),
            pl.BlockSpec((None, None, d, de), lambda t, te, nt, src: (layer, te[t], 0, 0)),
            pl.BlockSpec((None, None, d, de), lambda t, te, nt, src: (layer, te[t], 0, 0)),
            pl.BlockSpec((None, None, de, d), lambda t, te, nt, src: (layer, te[t], 0, 0)),
        ],
        out_specs=pl.BlockSpec((n_chunks, tm, LANES), lambda t, te, nt, src: (0, t, 0)),
        scratch_shapes=[
            pltpu.VMEM((2, n_chunks, tm, LANES), jnp.uint32),
            pltpu.SemaphoreType.DMA((2,)),
        ],
    )
    return pl.pallas_call(
        functools.partial(_ffn_kernel, tm=tm),
        name="moe_ffn",
        grid_spec=grid_spec,
        out_shape=jax.ShapeDtypeStruct((n_chunks, max_tiles * tm, LANES), jnp.uint32),
        compiler_params=_params(("arbitrary",), disable_bounds_checks=True),
    )(tile_expert, n_tiles, src_token, h2, w1, w3, w2)


def _combine_kernel(pos_ref, y_hbm, x_ref, mod_ref, w_ref, o_ref, ybuf, sem, *, tc):
    t = pl.program_id(0)
    n_steps = pl.num_programs(0)
    slot = t % 2

    def start_step(step, s):
        _start_row_gather(pos_ref, step * (TOP_K * tc), TOP_K * tc, y_hbm, ybuf.at[s], sem.at[s])

    @pl.when(t == 0)
    def _():
        start_step(0, 0)

    @pl.when(t + 1 < n_steps)
    def _():
        start_step(t + 1, 1 - slot)

    _wait_row_gather(y_hbm, ybuf.at[slot], sem.at[slot])
    half = ybuf.shape[1] * ybuf.shape[3]
    chunk = _tile(tc, 2 * SUBLANES)
    for r0 in range(0, tc, chunk):
        w = w_ref[r0:r0 + chunk, :]
        first = _unpack_halves(_load_tiled_rows(ybuf.at[slot, :, r0:r0 + chunk, :]))
        second = _unpack_halves(_load_tiled_rows(ybuf.at[slot, :, tc + r0:tc + r0 + chunk, :]))
        for part in range(2):
            cols = slice(part * half, (part + 1) * half)
            moe = w[:, 2:3] * first[part] + w[:, 3:4] * second[part]
            o_ref[r0:r0 + chunk, cols] = x_ref[r0:r0 + chunk, cols] + mod_ref[5:6, cols] * moe


def _combine(pos_steps, y_sorted, x2, mod_l, route, seq, tc):
    n, d = x2.shape
    per_seq = seq // tc
    grid_spec = pltpu.PrefetchScalarGridSpec(
        num_scalar_prefetch=1,
        grid=(n // tc,),
        in_specs=[
            pl.BlockSpec(memory_space=pl.ANY),
            pl.BlockSpec((tc, d), lambda t, pos: (t, 0)),
            pl.BlockSpec((None, 6, d), lambda t, pos: (t // per_seq, 0, 0)),
            pl.BlockSpec((tc, LANES), lambda t, pos: (t, 0)),
        ],
        out_specs=pl.BlockSpec((tc, d), lambda t, pos: (t, 0)),
        scratch_shapes=[
            pltpu.VMEM((2, y_sorted.shape[0], TOP_K * tc, LANES), jnp.uint32),
            pltpu.SemaphoreType.DMA((2,)),
        ],
    )
    return pl.pallas_call(
        functools.partial(_combine_kernel, tc=tc),
        name="moe_combine",
        grid_spec=grid_spec,
        out_shape=jax.ShapeDtypeStruct((n, d), F32),
        compiler_params=_params(("arbitrary",), disable_bounds_checks=True),
    )(pos_steps, y_sorted, x2, mod_l, route)


def _moe_plan(route, n_exp, tm, tc):
    n = route.shape[0]
    eid = route[:, 0:TOP_K].astype(jnp.int32)
    flat_e = eid.reshape(-1)
    onehot = (flat_e[:, None] == jnp.arange(n_exp, dtype=jnp.int32)[None, :]).astype(jnp.int32)
    csum = jnp.cumsum(onehot, axis=0)
    rank = jnp.sum(csum * onehot, axis=1) - 1
    counts = csum[-1]
    tiles = (counts + tm - 1) // tm
    tile_end = jnp.cumsum(tiles)
    tile_start = tile_end - tiles
    n_tiles = tile_end[-1]
    pos = jnp.sum(onehot * (tile_start * tm)[None, :], axis=1) + rank
    max_tiles = (TOP_K * n) // tm + n_exp
    token = jnp.arange(TOP_K * n, dtype=jnp.int32) // TOP_K
    src_token = jnp.zeros((max_tiles * tm,), jnp.int32).at[pos].set(token)
    tix = jnp.minimum(jnp.arange(max_tiles, dtype=jnp.int32), n_tiles - 1)
    tile_expert = jnp.sum((tile_end[None, :] <= tix[:, None]).astype(jnp.int32), axis=1)
    pos_steps = pos.reshape(n // tc, tc, TOP_K).transpose(0, 2, 1).reshape(-1)
    return tile_expert, n_tiles.reshape(1).astype(jnp.int32), src_token, pos_steps.astype(jnp.int32)


def kernel(x, c, positions, ada_w, ada_b, norm1_g, norm2_g, w_in, b_gate, lru_conv_w, lru_conv_b, lru_wr, lru_br, lru_wi, lru_bi, lru_lambda, pool_w, pool_b, pool_scale, q_norm_g, k_norm_g, lam_q1, lam_k1, lam_q2, lam_k2, subln_g, cv_dw_w, cv_dw_b, cv_ln_g, cv_ln_b, w_branch, w_out, router_g_w, router_g_b, router_e_w, router_e_b, moe_w1, moe_w3, moe_w2):
    bsz, seq, d = x.shape
    depth = ada_w.shape[0]
    db = lru_conv_w.shape[2]
    n_branch = w_branch.shape[1]
    n_mix = w_in.shape[2] - n_branch * d
    assert n_mix == 8 * db and q_norm_g.shape[1] == HEAD_DIM
    n_groups = router_g_w.shape[2]
    n_exp = router_e_w.shape[2]
    per_group = n_exp // n_groups
    assert n_groups + n_exp <= LANES
    n = bsz * seq
    col = {name: idx for idx, name in enumerate(("lru_x", "lru_gate", "pool", "q", "k", "v", "glu_val", "glu_gate"))}

    mod = _ada_mod(c, ada_w, ada_b).reshape(depth, bsz, 6, d)
    cos_t, sa_t, sb_t = _rope_tables(positions)
    seg = jnp.kron(jnp.eye(HEAD_W // HEAD_DIM, dtype=F32), jnp.full((HEAD_DIM, HEAD_DIM), 1.0 / HEAD_DIM, F32)).astype(BF16)
    dup = lambda g: jnp.concatenate([g, g]).reshape(1, HEAD_W)
    ffn_tm = _tile(TOP_K * n, 256)
    comb_tc = _tile(seq, 256)

    w_in16, w_branch16, w_out16 = w_in.astype(BF16), w_branch.astype(BF16), w_out.astype(BF16)
    moe_w1_16, moe_w3_16, moe_w2_16 = moe_w1.astype(BF16), moe_w3.astype(BF16), moe_w2.astype(BF16)

    x2 = x.reshape(n, d)
    for l in range(depth):
        mod_l = mod[l]
        h = _norm(x2, norm1_g[l], mod_l, seq, 0, 1)
        proj = _matmul(h, w_in16, l, n_mix)
        y_a = _lru_branch(proj, bsz, seq, col["lru_x"], col["lru_gate"], lru_conv_w[l], lru_conv_b[l],
                          lru_wr[l], lru_br[l], lru_wi[l], lru_bi[l], lru_lambda[l])
        y_b = _pool_branch(proj, bsz, seq, col["pool"], pool_w[l], pool_b[l], pool_scale[l])
        qn, kt = _qk_prep(proj, bsz, seq, col["q"], col["k"], db, cos_t, sa_t, sb_t, seg,
                          dup(q_norm_g[l]), dup(k_norm_g[l]))
        lambda_init = 0.8 - 0.6 * math.exp(-0.3 * l)
        y_c = _flash(qn, kt, proj, bsz, seq, col["v"], lam_q1[l], lam_k1[l], lam_q2[l], lam_k2[l],
                     subln_g[l], lambda_init)
        y_d = _conv_branch(proj, bsz, seq, col["glu_val"], col["glu_gate"], cv_dw_w[l], cv_dw_b[l],
                           cv_ln_g[l], cv_ln_b[l])
        merged = _merge(h, (y_a, y_b, y_c, y_d), w_in16, l, n_mix, b_gate[l], w_branch16)
        x2 = _resid_matmul(merged, w_out16, l, x2, mod_l, seq, 2)

        w_router = jnp.concatenate([router_g_w[l], router_e_w[l]], axis=1)
        w_router = jnp.pad(w_router, ((0, 0), (0, LANES - w_router.shape[1]))).astype(BF16)
        b_router = jnp.pad(jnp.concatenate([router_g_b[l], router_e_b[l]]), (0, LANES - n_groups - n_exp)).reshape(1, LANES)
        h2, route = _router(x2, norm2_g[l], mod_l, seq, w_router, b_router, n_groups, per_group)
        tile_expert, n_tiles, src_token, pos_steps = _moe_plan(route, n_exp, ffn_tm, comb_tc)
        y_sorted = _ffn(h2, tile_expert, n_tiles, src_token, moe_w1_16, moe_w3_16, moe_w2_16, l, ffn_tm)
        x2 = _combine(pos_steps, y_sorted, x2, mod_l, route, seq, comb_tc)
    return x2.reshape(bsz, seq, d)
```

```python
import functools
import math

import jax
import jax.numpy as jnp
from jax import lax
from jax.experimental import pallas as pl
from jax.experimental.pallas import tpu as pltpu

EPS = 1e-6
LRU_C = 8.0
LRU_CONV_TAPS = 4
POOL_WINDOWS = (2, 4, 8, 16)
POOL_HALO = 16
CONV_HALO = 32
HEAD_DIM = 64
HEAD_W = 2 * HEAD_DIM
ROPE_DIM = HEAD_DIM // 4
ROPE_HALF = ROPE_DIM // 2
ROPE_THETA = 500000.0
TOP_K = 2
NEG_BIG = -1e30
LOG2_E = math.log2(math.e)
SUBLANES = 8
LANES = 128
VMEM_LIMIT_BYTES = 56 * 1024 * 1024

BF16 = jnp.bfloat16
F32 = jnp.float32


def _tile(n, pref):
    t = min(n, pref)
    assert n % t == 0, (n, pref)
    return t


def _params(sem, **kw):
    return pltpu.CompilerParams(dimension_semantics=sem, vmem_limit_bytes=VMEM_LIMIT_BYTES, **kw)


def _sigmoid(x):
    return 1.0 / (1.0 + jnp.exp(-x))


def _ada_kernel(c_ref, w_ref, b_ref, o_ref):
    c = c_ref[...]
    c_act = (c * _sigmoid(c)).astype(BF16)
    o_ref[...] = jnp.dot(c_act, w_ref[...].astype(BF16), preferred_element_type=F32) + b_ref[...]


def _ada_mod(c, ada_w, ada_b):
    depth, d, n_out = ada_w.shape
    bsz = c.shape[0]
    tn = _tile(n_out, 1024)
    return pl.pallas_call(
        _ada_kernel,
        name="ada_mod",
        grid=(depth, n_out // tn),
        in_specs=[
            pl.BlockSpec((bsz, d), lambda l, j: (0, 0)),
            pl.BlockSpec((None, d, tn), lambda l, j: (l, 0, j)),
            pl.BlockSpec((None, 1, tn), lambda l, j: (l, 0, j)),
        ],
        out_specs=pl.BlockSpec((None, bsz, tn), lambda l, j: (l, 0, j)),
        out_shape=jax.ShapeDtypeStruct((depth, bsz, n_out), F32),
        compiler_params=_params(("arbitrary", "arbitrary")),
    )(c, ada_w, ada_b.reshape(depth, 1, n_out))


def _modnorm(x, g, mod, shift_row, scale_row):
    y = x * lax.rsqrt(jnp.mean(x * x, axis=-1, keepdims=True) + EPS) * g
    return y * (1.0 + mod[scale_row:scale_row + 1, :]) + mod[shift_row:shift_row + 1, :]


def _norm_kernel(x_ref, g_ref, mod_ref, o_ref, *, shift_row, scale_row):
    o_ref[...] = _modnorm(x_ref[...], g_ref[...], mod_ref[...], shift_row, scale_row).astype(o_ref.dtype)


def _norm(x2, g, mod_l, seq, shift_row, scale_row):
    n, d = x2.shape
    tm = _tile(seq, 512)
    per_seq = seq // tm
    return pl.pallas_call(
        functools.partial(_norm_kernel, shift_row=shift_row, scale_row=scale_row),
        name="mod_norm",
        grid=(n // tm,),
        in_specs=[
            pl.BlockSpec((tm, d), lambda i: (i, 0)),
            pl.BlockSpec((1, d), lambda i: (0, 0)),
            pl.BlockSpec((None, 6, d), lambda i: (i // per_seq, 0, 0)),
        ],
        out_specs=pl.BlockSpec((tm, d), lambda i: (i, 0)),
        out_shape=jax.ShapeDtypeStruct((n, d), BF16),
        compiler_params=_params(("arbitrary",)),
    )(x2, g.reshape(1, d), mod_l)


def _matmul_kernel(a_ref, w_ref, o_ref):
    o_ref[...] = jnp.dot(a_ref[...], w_ref[...], preferred_element_type=F32).astype(o_ref.dtype)


def _matmul(a, w, layer, n_cols):
    m, k = a.shape
    tm = _tile(m, 1024)
    tn = _tile(n_cols, 1024)
    return pl.pallas_call(
        _matmul_kernel,
        name="in_proj",
        grid=(m // tm, n_cols // tn),
        in_specs=[
            pl.BlockSpec((tm, k), lambda i, j: (i, 0)),
            pl.BlockSpec((None, k, tn), lambda i, j: (layer, 0, j)),
        ],
        out_specs=pl.BlockSpec((tm, tn), lambda i, j: (i, j)),
        out_shape=jax.ShapeDtypeStruct((m, n_cols), BF16),
        compiler_params=_params(("arbitrary", "arbitrary")),
    )(a, w)


def _gelu_tanh(x):
    return 0.5 * x * (1.0 + jnp.tanh(math.sqrt(2.0 / math.pi) * (x + 0.044715 * (x * x * x))))


def _lru_kernel(x_ref, gate_ref, cw_ref, cb_ref, wr_ref, br_ref, wi_ref, bi_ref, lam_ref, o_ref,
                xe_ref, a_ref, u_ref, carry_ref):
    ts, db = x_ref.shape
    n_blocks, blk = wr_ref.shape[0], wr_ref.shape[1]
    halo = SUBLANES
    first = pl.program_id(1) == 0

    @pl.when(first)
    def _():
        xe_ref[0:halo, :] = jnp.zeros((halo, db), F32)
        carry_ref[...] = jnp.zeros_like(carry_ref)

    @pl.when(jnp.logical_not(first))
    def _():
        xe_ref[0:halo, :] = xe_ref[ts:ts + halo, :]

    xe_ref[halo:halo + ts, :] = x_ref[...].astype(F32)

    xa = jnp.zeros((ts, db), F32) + cb_ref[...]
    for j in range(LRU_CONV_TAPS):
        off = halo - (LRU_CONV_TAPS - 1) + j
        xa = xa + cw_ref[j:j + 1, :] * xe_ref[off:off + ts, :]

    xa16 = xa.astype(BF16)
    for hb in range(n_blocks):
        sl = slice(hb * blk, (hb + 1) * blk)
        a_ref[:, sl] = jnp.dot(xa16[:, sl], wr_ref[hb], preferred_element_type=F32)
        u_ref[:, sl] = jnp.dot(xa16[:, sl], wi_ref[hb], preferred_element_type=F32)
    r = _sigmoid(a_ref[...] + br_ref[...])
    ig = _sigmoid(u_ref[...] + bi_ref[...])
    z = -lam_ref[...]
    softplus = jnp.maximum(z, 0.0) + jnp.log(1.0 + jnp.exp(-jnp.abs(z)))
    a = jnp.exp((-LRU_C) * r * softplus)
    u = jnp.sqrt(1.0 - a * a) * (ig * xa)

    row = lax.broadcasted_iota(jnp.int32, (ts, db), 0) % SUBLANES
    for d in (1, 2, 4):
        keep = row >= d
        a_sh = jnp.where(keep, pltpu.roll(a, d, 0), 1.0)
        u_sh = jnp.where(keep, pltpu.roll(u, d, 0), 0.0)
        u = a * u_sh + u
        a = a * a_sh
    a_ref[...] = a
    u_ref[...] = u

    def body(c, h_prev):
        r0 = pl.multiple_of(c * SUBLANES, SUBLANES)
        h8 = u_ref[pl.ds(r0, SUBLANES), :] + a_ref[pl.ds(r0, SUBLANES), :] * h_prev
        u_ref[pl.ds(r0, SUBLANES), :] = h8
        return h8[SUBLANES - 1:SUBLANES, :]

    carry_ref[...] = lax.fori_loop(0, ts // SUBLANES, body, carry_ref[...])
    o_ref[...] = (_gelu_tanh(gate_ref[...].astype(F32)) * u_ref[...]).astype(o_ref.dtype)


def _lru_branch(proj, bsz, seq, col_x, col_gate, cw, cb, wr, br, wi, bi, lam):
    db = cw.shape[1]
    ts = _tile(seq, 256)
    per_seq = seq // ts
    n_blocks, blk = wr.shape[0], wr.shape[1]
    row = lambda v: v.reshape(1, db)
    vec = pl.BlockSpec((1, db), lambda b, i: (0, 0))
    wspec = pl.BlockSpec((n_blocks, blk, blk), lambda b, i: (0, 0, 0))
    return pl.pallas_call(
        _lru_kernel,
        name="lru_branch",
        grid=(bsz, per_seq),
        in_specs=[
            pl.BlockSpec((ts, db), lambda b, i: (b * per_seq + i, col_x)),
            pl.BlockSpec((ts, db), lambda b, i: (b * per_seq + i, col_gate)),
            pl.BlockSpec((LRU_CONV_TAPS, db), lambda b, i: (0, 0)),
            vec, wspec, vec, wspec, vec, vec,
        ],
        out_specs=pl.BlockSpec((ts, db), lambda b, i: (b * per_seq + i, 0)),
        out_shape=jax.ShapeDtypeStruct((bsz * seq, db), BF16),
        scratch_shapes=[
            pltpu.VMEM((ts + SUBLANES, db), F32),
            pltpu.VMEM((ts, db), F32),
            pltpu.VMEM((ts, db), F32),
            pltpu.VMEM((1, db), F32),
        ],
        compiler_params=_params(("arbitrary", "arbitrary")),
    )(proj, proj, cw, row(cb), wr.astype(BF16), row(br), wi.astype(BF16), row(bi), row(lam))


def _pool_kernel(x_ref, w_ref, b_ref, s_ref, o_ref, xe_ref):
    ts, db = x_ref.shape
    n_groups, pg = w_ref.shape[0], w_ref.shape[1]
    halo = POOL_HALO
    i = pl.program_id(1)

    @pl.when(i == 0)
    def _():
        xe_ref[0:halo, :] = jnp.zeros((halo, db), F32)

    @pl.when(i != 0)
    def _():
        xe_ref[0:halo, :] = xe_ref[ts:ts + halo, :]

    xe_ref[halo:halo + ts, :] = x_ref[...].astype(F32)
    t1 = (i * ts + 1 + lax.broadcasted_iota(jnp.int32, (ts, pg), 0)).astype(F32)
    for gi in range(n_groups):
        win = POOL_WINDOWS[gi]
        sl = slice(gi * pg, (gi + 1) * pg)
        x = xe_ref[halo:halo + ts, sl]
        acc = x
        for j in range(1, win):
            acc = acc + xe_ref[halo - j:halo - j + ts, sl]
        diff = acc / jnp.minimum(t1, float(win)) - x
        y = jnp.dot(diff.astype(BF16), w_ref[gi], preferred_element_type=F32) + b_ref[:, sl]
        o_ref[:, sl] = (y * s_ref[:, sl]).astype(o_ref.dtype)


def _pool_branch(proj, bsz, seq, col, w, b, scale):
    n_groups, pg = w.shape[0], w.shape[1]
    assert n_groups == len(POOL_WINDOWS)
    db = n_groups * pg
    ts = _tile(seq, 512)
    per_seq = seq // ts
    vec = pl.BlockSpec((1, db), lambda bb, i: (0, 0))
    return pl.pallas_call(
        _pool_kernel,
        name="pool_branch",
        grid=(bsz, per_seq),
        in_specs=[
            pl.BlockSpec((ts, db), lambda bb, i: (bb * per_seq + i, col)),
            pl.BlockSpec((n_groups, pg, pg), lambda bb, i: (0, 0, 0)),
            vec, vec,
        ],
        out_specs=pl.BlockSpec((ts, db), lambda bb, i: (bb * per_seq + i, 0)),
        out_shape=jax.ShapeDtypeStruct((bsz * seq, db), BF16),
        scratch_shapes=[pltpu.VMEM((ts + POOL_HALO, db), F32)],
        compiler_params=_params(("arbitrary", "arbitrary")),
    )(proj, w.astype(BF16), b.reshape(1, db), scale.reshape(1, db))


def _conv_kernel(val_ref, gate_ref, w_ref, b_ref, g_ref, beta_ref, o_ref, ue_ref, sh_ref, *, row_block):
    ts, db = val_ref.shape
    taps = w_ref.shape[0]
    halo = CONV_HALO
    i = pl.program_id(1)

    @pl.when(i == 0)
    def _():
        ue_ref[0:halo, :] = jnp.zeros((halo, db), F32)

    @pl.when(i != 0)
    def _():
        ue_ref[0:halo, :] = ue_ref[ts:ts + halo, :]

    ue_ref[halo:halo + ts, :] = val_ref[...].astype(F32) * _sigmoid(gate_ref[...].astype(F32))
    span = sh_ref.shape[1]
    for p in range(1, SUBLANES):
        sh_ref[p - 1] = ue_ref[p:p + span, :]
    base = halo - (taps - 1)
    for rb in range(ts // row_block):
        r0 = rb * row_block
        acc = jnp.zeros((row_block, db), F32) + b_ref[...]
        for j in range(taps):
            a, p = divmod(base + j, SUBLANES)
            row = r0 + a * SUBLANES
            window = ue_ref[row:row + row_block, :] if p == 0 else sh_ref[p - 1, row:row + row_block, :]
            acc = acc + w_ref[j:j + 1, :] * window
        mu = jnp.mean(acc, axis=-1, keepdims=True)
        cen = acc - mu
        var = jnp.mean(cen * cen, axis=-1, keepdims=True)
        y = cen * lax.rsqrt(var + EPS) * g_ref[...] + beta_ref[...]
        o_ref[r0:r0 + row_block, :] = (y * _sigmoid(y)).astype(o_ref.dtype)


def _conv_branch(proj, bsz, seq, col_val, col_gate, w, b, g, beta):
    taps, db = w.shape
    assert taps - 1 <= CONV_HALO
    ts = _tile(seq, 256)
    per_seq = seq // ts
    row = lambda v: v.reshape(1, db)
    vec = pl.BlockSpec((1, db), lambda bb, i: (0, 0))
    return pl.pallas_call(
        functools.partial(_conv_kernel, row_block=_tile(ts, 32)),
        name="conv_branch",
        grid=(bsz, per_seq),
        in_specs=[
            pl.BlockSpec((ts, db), lambda bb, i: (bb * per_seq + i, col_val)),
            pl.BlockSpec((ts, db), lambda bb, i: (bb * per_seq + i, col_gate)),
            pl.BlockSpec((taps, db), lambda bb, i: (0, 0)),
            vec, vec, vec,
        ],
        out_specs=pl.BlockSpec((ts, db), lambda bb, i: (bb * per_seq + i, 0)),
        out_shape=jax.ShapeDtypeStruct((bsz * seq, db), BF16),
        scratch_shapes=[
            pltpu.VMEM((ts + CONV_HALO, db), F32),
            pltpu.VMEM((SUBLANES - 1, ts + CONV_HALO - SUBLANES, db), F32),
        ],
        compiler_params=_params(("arbitrary", "arbitrary")),
    )(proj, proj, w, row(b), row(g), row(beta))


def _qk_prep_kernel(q_ref, k_ref, cos_ref, sa_ref, sb_ref, seg_ref, qg_ref, kg_ref, qo_ref, kt_ref, kn_ref):
    ts, db = q_ref.shape
    cos, sa, sb = cos_ref[...], sa_ref[...], sb_ref[...]

    def norm_rope(x, g):
        ms = jnp.dot((x * x).astype(BF16), seg_ref[...], preferred_element_type=F32)
        xn = x * lax.rsqrt(ms + EPS) * g
        return (xn * cos + pltpu.roll(xn, HEAD_W - ROPE_HALF, 1) * sa + pltpu.roll(xn, ROPE_HALF, 1) * sb)

    for h in range(db // HEAD_W):
        sl = slice(h * HEAD_W, (h + 1) * HEAD_W)
        q = norm_rope(q_ref[:, sl].astype(F32), qg_ref[...])
        qo_ref[:, sl] = (q * (LOG2_E / math.sqrt(HEAD_DIM))).astype(qo_ref.dtype)
        kn_ref[:, sl] = norm_rope(k_ref[:, sl].astype(F32), kg_ref[...])
    kt_ref[...] = kn_ref[...].T.astype(kt_ref.dtype)


def _qk_prep(proj, bsz, seq, col_q, col_k, db, cos_t, sa_t, sb_t, seg, qg, kg):
    ts = _tile(seq, 256)
    per_seq = seq // ts
    tab = pl.BlockSpec((ts, HEAD_W), lambda b, i: (b * per_seq + i, 0))
    vec = pl.BlockSpec((1, HEAD_W), lambda b, i: (0, 0))
    return pl.pallas_call(
        _qk_prep_kernel,
        name="qk_prep",
        grid=(bsz, per_seq),
        in_specs=[
            pl.BlockSpec((ts, db), lambda b, i: (b * per_seq + i, col_q)),
            pl.BlockSpec((ts, db), lambda b, i: (b * per_seq + i, col_k)),
            tab, tab, tab,
            pl.BlockSpec((HEAD_W, HEAD_W), lambda b, i: (0, 0)),
            vec, vec,
        ],
        out_specs=[
            pl.BlockSpec((ts, db), lambda b, i: (b * per_seq + i, 0)),
            pl.BlockSpec((None, db, ts), lambda b, i: (b, 0, i)),
        ],
        out_shape=[
            jax.ShapeDtypeStruct((bsz * seq, db), BF16),
            jax.ShapeDtypeStruct((bsz, db, seq), BF16),
        ],
        scratch_shapes=[pltpu.VMEM((ts, db), F32)],
        compiler_params=_params(("arbitrary", "arbitrary")),
    )(proj, proj, cos_t, sa_t, sb_t, seg, qg, kg)


def _flash_kernel(q_ref, kt_ref, v_ref, lq1_ref, lk1_ref, lq2_ref, lk2_ref, sg_ref, o_ref,
                  m_ref, l_ref, acc_ref, p_ref, s_ref, *, lambda_init, tk):
    tq = q_ref.shape[0]
    rows = 2 * tq
    i = pl.program_id(2)
    lane = lax.broadcasted_iota(jnp.int32, (tq, HEAD_W), 1)
    q = q_ref[...]
    zero = jnp.zeros_like(q)
    q2 = jnp.concatenate([jnp.where(lane < HEAD_DIM, q, zero), jnp.where(lane >= HEAD_DIM, q, zero)], axis=0)
    m_ref[...] = jnp.full(m_ref.shape, NEG_BIG, F32)
    l_ref[...] = jnp.zeros(l_ref.shape, F32)
    acc_ref[...] = jnp.zeros(acc_ref.shape, F32)
    p_ref[...] = jnp.zeros(p_ref.shape, p_ref.dtype)

    def pv(j):
        k0 = pl.multiple_of(j * tk, tk)
        return jnp.dot(p_ref[...], v_ref[pl.ds(k0, tk), :], preferred_element_type=F32)

    def scores(j):
        k0 = pl.multiple_of(j * tk, tk)
        return jnp.dot(q2, kt_ref[:, pl.ds(k0, tk)], preferred_element_type=F32)

    n_full = (i * tq) // tk
    n_all = ((i + 1) * tq + tk - 1) // tk
    s_ref[...] = scores(0)

    def step(j, masked):
        k0 = pl.multiple_of(j * tk, tk)
        s = s_ref[...]
        pv_prev = pv(jnp.maximum(j - 1, 0))
        s_ref[...] = scores(jnp.minimum(j + 1, n_all - 1))
        if masked:
            qpos = i * tq + lax.broadcasted_iota(jnp.int32, (rows, tk), 0) % tq
            kpos = k0 + lax.broadcasted_iota(jnp.int32, (rows, tk), 1)
            s = jnp.where(kpos <= qpos, s, NEG_BIG)
        cols = [s[:, cc * LANES:(cc + 1) * LANES] for cc in range(tk // LANES)]
        m_prev = m_ref[...]
        m_next = jnp.maximum(m_prev, jnp.max(functools.reduce(jnp.maximum, cols), axis=-1, keepdims=True))
        alpha = jnp.exp2(m_prev - m_next)
        ps = [jnp.exp2(col - m_next) for col in cols]
        l_ref[...] = alpha * l_ref[...] + functools.reduce(jnp.add, ps)
        acc_ref[...] = alpha * (acc_ref[...] + pv_prev)
        p_ref[...] = jnp.concatenate([x.astype(p_ref.dtype) for x in ps], axis=1)
        m_ref[...] = m_next

    def full_body(j, c):
        step(j, False)
        return c

    def diag_body(j, c):
        step(j, True)
        return c

    lax.fori_loop(0, n_full, full_body, 0)
    lax.fori_loop(n_full, n_all, diag_body, 0)

    lam = (jnp.exp(jnp.sum(lq1_ref[...] * lk1_ref[...], axis=-1, keepdims=True))
           - jnp.exp(jnp.sum(lq2_ref[...] * lk2_ref[...], axis=-1, keepdims=True)) + lambda_init)
    o_all = (acc_ref[...] + pv(n_all - 1)) / jnp.sum(l_ref[...], axis=-1, keepdims=True)
    o = o_all[0:tq, :] - lam * o_all[tq:rows, :]
    o = o * lax.rsqrt(jnp.mean(o * o, axis=-1, keepdims=True) + EPS) * sg_ref[...]
    o_ref[...] = (o * (1.0 - lambda_init)).astype(o_ref.dtype)


def _flash(qn, kt, proj, bsz, seq, col_v, lq1, lk1, lq2, lk2, sub_g, lambda_init):
    db = qn.shape[1]
    heads = db // HEAD_W
    tq = _tile(seq, 512)
    tk = _tile(seq, 512)
    per_seq = seq // tq
    vec64 =pl.BlockSpec((1, HEAD_DIM), lambda b, h, i: (0, 0))
    row64 = lambda v: v.reshape(1, HEAD_DIM)
    return pl.pallas_call(
        functools.partial(_flash_kernel, lambda_init=lambda_init, tk=tk),
        name="diff_attn",
        grid=(bsz, heads, per_seq),
        in_specs=[
            pl.BlockSpec((tq, HEAD_W), lambda b, h, i: (b * per_seq + i, h)),
            pl.BlockSpec((None, HEAD_W, seq), lambda b, h, i: (b, h, 0)),
            pl.BlockSpec((seq, HEAD_W), lambda b, h, i: (b, col_v * heads + h)),
            vec64, vec64, vec64, vec64,
            pl.BlockSpec((1, HEAD_W), lambda b, h, i: (0, 0)),
        ],
        out_specs=pl.BlockSpec((tq, HEAD_W), lambda b, h, i: (b * per_seq + i, h)),
        out_shape=jax.ShapeDtypeStruct((bsz * seq, db), BF16),
        scratch_shapes=[pltpu.VMEM((2 * tq, HEAD_W), F32)] * 3 + [
            pltpu.VMEM((2 * tq, tk), BF16),
            pltpu.VMEM((2 * tq, tk), F32),
        ],
        compiler_params=_params(("arbitrary", "arbitrary", "arbitrary")),
    )(qn, kt, proj, row64(lq1), row64(lk1), row64(lq2), row64(lk2), sub_g.reshape(1, HEAD_W))


def _rope_tables(positions):
    inv_freq = ROPE_THETA ** (-jnp.arange(0, ROPE_DIM, 2, dtype=F32) / ROPE_DIM)
    lane = jnp.arange(HEAD_W, dtype=jnp.int32) % HEAD_DIM
    freq = jnp.where(lane < ROPE_DIM, jnp.take(inv_freq, lane % ROPE_HALF), 0.0)
    ang = positions.astype(F32).reshape(-1, 1) * freq[None, :]
    sin = jnp.sin(ang)
    return jnp.cos(ang), jnp.where(lane < ROPE_HALF, -sin, 0.0), jnp.where(lane >= ROPE_HALF, sin, 0.0)


def _merge_kernel(h_ref, ya_ref, yb_ref, yc_ref, yd_ref, wa_ref, wb_ref, wc_ref, wd_ref,
                  bg_ref, wbr_ref, o_ref):
    h = h_ref[...]
    acc = None
    for bi, (y_ref, wg_ref) in enumerate(((ya_ref, wa_ref), (yb_ref, wb_ref), (yc_ref, wc_ref), (yd_ref, wd_ref))):
        gate = _sigmoid(jnp.dot(h, wg_ref[...], preferred_element_type=F32) + bg_ref[bi:bi + 1, :])
        term = gate * jnp.dot(y_ref[...], wbr_ref[bi], preferred_element_type=F32)
        acc = term if acc is None else acc + term
    o_ref[...] = acc.astype(o_ref.dtype)


def _merge(h, ys, w_in16, layer, n_mix, b_gate, w_branch16):
    n, d = h.shape
    db = ys[0].shape[1]
    n_branch = len(ys)
    tm = _tile(n, 1024)
    tn = _tile(d, 256)
    gate_blk = [(n_mix + b * d) // tn for b in range(n_branch)]
    once = pl.Buffered(1)
    wspecs = [pl.BlockSpec((None, d, tn), functools.partial(lambda j, i, base: (layer, 0, base + j), base=gb),
                           pipeline_mode=once) for gb in gate_blk]
    yspec = pl.BlockSpec((tm, db), lambda j, i: (i, 0))
    return pl.pallas_call(
        _merge_kernel,
        name="gated_merge",
        grid=(d // tn, n // tm),
        in_specs=[pl.BlockSpec((tm, d), lambda j, i: (i, 0))] + [yspec] * n_branch + wspecs + [
            pl.BlockSpec((n_branch, tn), lambda j, i: (0, j)),
            pl.BlockSpec((None, n_branch, db, tn), lambda j, i: (layer, 0, 0, j), pipeline_mode=once),
        ],
        out_specs=pl.BlockSpec((tm, tn), lambda j, i: (i, j)),
        out_shape=jax.ShapeDtypeStruct((n, d), BF16),
        compiler_params=_params(("arbitrary", "arbitrary")),
    )(h, *ys, *([w_in16] * n_branch), b_gate, w_branch16)


def _resid_matmul_kernel(a_ref, w_ref, x_ref, mod_ref, o_ref, *, gate_row):
    y = jnp.dot(a_ref[...], w_ref[...], preferred_element_type=F32)
    o_ref[...] = x_ref[...] + mod_ref[gate_row:gate_row + 1, :] * y


def _resid_matmul(a, w16, layer, x2, mod_l, seq, gate_row):
    n, k = a.shape
    d = w16.shape[2]
    tm = _tile(seq, 1024)
    tn = _tile(d, 1024)
    per_seq = seq // tm
    return pl.pallas_call(
        functools.partial(_resid_matmul_kernel, gate_row=gate_row),
        name="out_proj_resid",
        grid=(n // tm, d // tn),
        in_specs=[
            pl.BlockSpec((tm, k), lambda i, j: (i, 0)),
            pl.BlockSpec((None, k, tn), lambda i, j: (layer, 0, j)),
            pl.BlockSpec((tm, tn), lambda i, j: (i, j)),
            pl.BlockSpec((None, 6, tn), lambda i, j: (i // per_seq, 0, j)),
        ],
        out_specs=pl.BlockSpec((tm, tn), lambda i, j: (i, j)),
        out_shape=jax.ShapeDtypeStruct((n, d), F32),
        compiler_params=_params(("arbitrary", "arbitrary")),
    )(a, w16, x2, mod_l)


def _pack_halves(v):
    w = v.shape[1] // 2
    bits = lambda t: pltpu.bitcast(t.astype(BF16).astype(F32), jnp.uint32)
    return bits(v[:, w:]) | (bits(v[:, :w]) >> 16)


def _unpack_halves(p):
    return (pltpu.bitcast(p << 16, F32), pltpu.bitcast(p & jnp.uint32(0xFFFF0000), F32))


def _store_tiled_rows(ref3, val2):
    for s in range(ref3.shape[0]):
        ref3[s] = val2[:, s * LANES:(s + 1) * LANES]


def _load_tiled_rows(ref3):
    return jnp.concatenate([ref3[s] for s in range(ref3.shape[0])], axis=1)


def _router_kernel(x_ref, g_ref, mod_ref, wr_ref, br_ref, h_ref, r_ref, *, n_groups, per_group):
    h2 = _modnorm(x_ref[...], g_ref[...], mod_ref[...], 3, 4)
    _store_tiled_rows(h_ref, _pack_halves(h2))
    logits = jnp.dot(h2.astype(BF16), wr_ref[...], preferred_element_type=F32) + br_ref[...]
    lane = lax.broadcasted_iota(jnp.int32, logits.shape, 1)
    n_exp = n_groups * per_group
    is_g = lane < n_groups
    gl = jnp.where(is_g, logits, NEG_BIG)
    gmax = jnp.max(gl, axis=-1, keepdims=True)
    g_idx = jnp.min(jnp.where(gl == gmax, lane, LANES), axis=-1, keepdims=True)
    g_top = 1.0 / jnp.sum(jnp.where(is_g, jnp.exp(gl - gmax), 0.0), axis=-1, keepdims=True)
    lo = n_groups + g_idx * per_group
    in_grp = jnp.logical_and(lane >= lo, lane < lo + per_group)
    el = jnp.where(in_grp, logits, NEG_BIG)
    v1 = jnp.max(el, axis=-1, keepdims=True)
    i1 = jnp.min(jnp.where(el == v1, lane, LANES), axis=-1, keepdims=True)
    el2 = jnp.where(lane == i1, NEG_BIG, el)
    v2 = jnp.max(el2, axis=-1, keepdims=True)
    i2 = jnp.min(jnp.where(el2 == v2, lane, LANES), axis=-1, keepdims=True)
    e = jnp.exp(v2 - v1)
    w1 = g_top / (1.0 + e)
    w2 = g_top * e / (1.0 + e)
    del n_exp
    out = jnp.where(lane == 0, (i1 - n_groups).astype(F32),
                    jnp.where(lane == 1, (i2 - n_groups).astype(F32),
                              jnp.where(lane == 2, w1, jnp.where(lane == 3, w2, 0.0))))
    r_ref[...] = out


def _router(x2, g, mod_l, seq, w_router16, b_router, n_groups, per_group):
    n, d = x2.shape
    tm = _tile(seq, 256)
    per_seq = seq // tm
    return pl.pallas_call(
        functools.partial(_router_kernel, n_groups=n_groups, per_group=per_group),
        name="moe_router",
        grid=(n // tm,),
        in_specs=[
            pl.BlockSpec((tm, d), lambda i: (i, 0)),
            pl.BlockSpec((1, d), lambda i: (0, 0)),
            pl.BlockSpec((None, 6, d), lambda i: (i // per_seq, 0, 0)),
            pl.BlockSpec((d, LANES), lambda i: (0, 0)),
            pl.BlockSpec((1, LANES), lambda i: (0, 0)),
        ],
        out_specs=[
            pl.BlockSpec((d // (2 * LANES), tm, LANES), lambda i: (0, i, 0)),
            pl.BlockSpec((tm, LANES), lambda i: (i, 0)),
        ],
        out_shape=[
            jax.ShapeDtypeStruct((d // (2 * LANES), n, LANES), jnp.uint32),
            jax.ShapeDtypeStruct((n, LANES), F32),
        ],
        compiler_params=_params(("arbitrary",)),
    )(x2, g.reshape(1, d), mod_l, w_router16, b_router)


GATHER_UNROLL = 8
FFN_GATHER_SLOTS = 3


def _start_row_gather(idx_ref, base, n_rows, src_hbm, dst, sem):
    def body(g, c):
        r0 = pl.multiple_of(g * GATHER_UNROLL, GATHER_UNROLL)
        for k in range(GATHER_UNROLL):
            row = idx_ref[base + r0 + k]
            pltpu.make_async_copy(src_hbm.at[:, pl.ds(row, 1), :], dst.at[:, pl.ds(r0 + k, 1), :],
                                  sem).start(priority=k % 2)
        return c
    lax.fori_loop(0, n_rows // GATHER_UNROLL, body, 0)


def _wait_row_gather(src_hbm, dst, sem):
    pltpu.make_async_copy(src_hbm.at[:, pl.ds(0, dst.shape[1]), :], dst, sem).wait()


def _ffn_kernel(tile_exp_ref, n_tiles_ref, src_ref, h_hbm, w1_ref, w3_ref, w2_ref, y_ref, xbuf, sem, *, tm):
    t = pl.program_id(0)
    n_tiles = n_tiles_ref[0]
    n_slots = xbuf.shape[0]
    ahead = n_slots - 1
    slot = t % n_slots

    def start_tile(tile):
        s = tile % n_slots
        _start_row_gather(src_ref, tile * tm, tm, h_hbm, xbuf.at[s], sem.at[s])

    for first in range(ahead):
        @pl.when(jnp.logical_and(t == 0, first < n_tiles))
        def _():
            start_tile(first)

    @pl.when(t + ahead < n_tiles)
    def _():
        start_tile(t + ahead)

    @pl.when(t < n_tiles)
    def _():
        _wait_row_gather(h_hbm, xbuf.at[slot], sem.at[slot])
        x_lo, x_hi = (v.astype(BF16) for v in _unpack_halves(_load_tiled_rows(xbuf.at[slot])))
        half = x_lo.shape[1]

        def up_proj(w_ref):
            return (jnp.dot(x_lo, w_ref[0:half, :], preferred_element_type=F32)
                    + jnp.dot(x_hi, w_ref[half:2 * half, :], preferred_element_type=F32))

        a = up_proj(w1_ref)
        b = up_proj(w3_ref)
        hid = (a * _sigmoid(a) * b).astype(BF16)
        _store_tiled_rows(y_ref, _pack_halves(jnp.dot(hid, w2_ref[...], preferred_element_type=F32)))

    @pl.when(t >= n_tiles)
    def _():
        y_ref[...] = jnp.zeros_like(y_ref)


def _ffn(h2, tile_expert, n_tiles, src_token, w1, w3, w2, layer, tm):
    d, de = w1.shape[2], w1.shape[3]
    n_chunks = h2.shape[0]
    max_tiles = tile_expert.shape[0]
    grid_spec = pltpu.PrefetchScalarGridSpec(
        num_scalar_prefetch=3,
        grid=(max_tiles,),
        in_specs=[
            pl.BlockSpec(memory_space=pl.ANY),
            pl.BlockSpec((None, None, d, de), lambda t, te, nt, src: (layer, te[t], 0, 0)),
            pl.BlockSpec((None, None, d, de), lambda t, te, nt, src: (layer, te[t], 0, 0)),
            pl.BlockSpec((None, None, de, d), lambda t, te, nt, src: (layer, te[t], 0, 0)),
        ],
        out_specs=pl.BlockSpec((n_chunks, tm, LANES), lambda t, te, nt, src: (0, t, 0)),
        scratch_shapes=[
            pltpu.VMEM((FFN_GATHER_SLOTS, n_chunks, tm, LANES), jnp.uint32),
            pltpu.SemaphoreType.DMA((FFN_GATHER_SLOTS,)),
        ],
    )
    return pl.pallas_call(
        functools.partial(_ffn_kernel, tm=tm),
        name="moe_ffn",
        grid_spec=grid_spec,
        out_shape=jax.ShapeDtypeStruct((n_chunks, max_tiles * tm, LANES), jnp.uint32),
        compiler_params=_params(("arbitrary",), disable_bounds_checks=True),
    )(tile_expert, n_tiles, src_token, h2, w1, w3, w2)


def _combine_kernel(pos_ref, y_hbm, x_ref, mod_ref, w_ref, o_ref, ybuf, sem, *, tc):
    t = pl.program_id(0)
    n_steps = pl.num_programs(0)
    slot = t % 2

    def start_step(step, s):
        _start_row_gather(pos_ref, step * (TOP_K * tc), TOP_K * tc, y_hbm, ybuf.at[s], sem.at[s])

    @pl.when(t == 0)
    def _():
        start_step(0, 0)

    @pl.when(t + 1 < n_steps)
    def _():
        start_step(t + 1, 1 - slot)

    _wait_row_gather(y_hbm, ybuf.at[slot], sem.at[slot])
    half = ybuf.shape[1] * ybuf.shape[3]
    chunk = _tile(tc, 2 * SUBLANES)
    for r0 in range(0, tc, chunk):
        w = w_ref[r0:r0 + chunk, :]
        first = _unpack_halves(_load_tiled_rows(ybuf.at[slot, :, r0:r0 + chunk, :]))
        second = _unpack_halves(_load_tiled_rows(ybuf.at[slot, :, tc + r0:tc + r0 + chunk, :]))
        for part in range(2):
            cols = slice(part * half, (part + 1) * half)
            moe = w[:, 2:3] * first[part] + w[:, 3:4] * second[part]
            o_ref[r0:r0 + chunk, cols] = x_ref[r0:r0 + chunk, cols] + mod_ref[5:6, cols] * moe


def _combine(pos_steps, y_sorted, x2, mod_l, route, seq, tc):
    n, d = x2.shape
    per_seq = seq // tc
    grid_spec = pltpu.PrefetchScalarGridSpec(
        num_scalar_prefetch=1,
        grid=(n // tc,),
        in_specs=[
            pl.BlockSpec(memory_space=pl.ANY),
            pl.BlockSpec((tc, d), lambda t, pos: (t, 0)),
            pl.BlockSpec((None, 6, d), lambda t, pos: (t // per_seq, 0, 0)),
            pl.BlockSpec((tc, LANES), lambda t, pos: (t, 0)),
        ],
        out_specs=pl.BlockSpec((tc, d), lambda t, pos: (t, 0)),
        scratch_shapes=[
            pltpu.VMEM((2, y_sorted.shape[0], TOP_K * tc, LANES), jnp.uint32),
            pltpu.SemaphoreType.DMA((2,)),
        ],
    )
    return pl.pallas_call(
        functools.partial(_combine_kernel, tc=tc),
        name="moe_combine",
        grid_spec=grid_spec,
        out_shape=jax.ShapeDtypeStruct((n, d), F32),
        compiler_params=_params(("arbitrary",), disable_bounds_checks=True),
    )(pos_steps, y_sorted, x2, mod_l, route)


def _moe_plan(route, n_exp, tm, tc):
    n = route.shape[0]
    eid = route[:, 0:TOP_K].astype(jnp.int32)
    flat_e = eid.reshape(-1)
    onehot = (flat_e[:, None] == jnp.arange(n_exp, dtype=jnp.int32)[None, :]).astype(jnp.int32)
    csum = jnp.cumsum(onehot, axis=0)
    rank = jnp.sum(csum * onehot, axis=1) - 1
    counts = csum[-1]
    tiles = (counts + tm - 1) // tm
    tile_end = jnp.cumsum(tiles)
    tile_start = tile_end - tiles
    n_tiles = tile_end[-1]
    pos = jnp.sum(onehot * (tile_start * tm)[None, :], axis=1) + rank
    max_tiles = (TOP_K * n) // tm + n_exp
    token = jnp.arange(TOP_K * n, dtype=jnp.int32) // TOP_K
    src_token = jnp.zeros((max_tiles * tm,), jnp.int32).at[pos].set(token)
    tix = jnp.minimum(jnp.arange(max_tiles, dtype=jnp.int32), n_tiles - 1)
    tile_expert = jnp.sum((tile_end[None, :] <= tix[:, None]).astype(jnp.int32), axis=1)
    pos_steps = pos.reshape(n // tc, tc, TOP_K).transpose(0, 2, 1).reshape(-1)
    return tile_expert, n_tiles.reshape(1).astype(jnp.int32), src_token, pos_steps.astype(jnp.int32)


def kernel(x, c, positions, ada_w, ada_b, norm1_g, norm2_g, w_in, b_gate, lru_conv_w, lru_conv_b, lru_wr, lru_br, lru_wi, lru_bi, lru_lambda, pool_w, pool_b, pool_scale, q_norm_g, k_norm_g, lam_q1, lam_k1, lam_q2, lam_k2, subln_g, cv_dw_w, cv_dw_b, cv_ln_g, cv_ln_b, w_branch, w_out, router_g_w, router_g_b, router_e_w, router_e_b, moe_w1, moe_w3, moe_w2):
    bsz, seq, d = x.shape
    depth = ada_w.shape[0]
    db = lru_conv_w.shape[2]
    n_branch = w_branch.shape[1]
    n_mix = w_in.shape[2] - n_branch * d
    assert n_mix == 8 * db and q_norm_g.shape[1] == HEAD_DIM
    n_groups = router_g_w.shape[2]
    n_exp = router_e_w.shape[2]
    per_group = n_exp // n_groups
    assert n_groups + n_exp <= LANES
    n = bsz * seq
    col = {name: idx for idx, name in enumerate(("lru_x", "lru_gate", "pool", "q", "k", "v", "glu_val", "glu_gate"))}

    mod = _ada_mod(c, ada_w, ada_b).reshape(depth, bsz, 6, d)
    cos_t, sa_t, sb_t = _rope_tables(positions)
    seg = jnp.kron(jnp.eye(HEAD_W // HEAD_DIM, dtype=F32), jnp.full((HEAD_DIM, HEAD_DIM), 1.0 / HEAD_DIM, F32)).astype(BF16)
    dup = lambda g: jnp.concatenate([g, g]).reshape(1, HEAD_W)
    ffn_tm = _tile(TOP_K * n, 256)
    comb_tc = _tile(seq, 256)

    w_in16, w_branch16, w_out16 = w_in.astype(BF16), w_branch.astype(BF16), w_out.astype(BF16)
    moe_w1_16, moe_w3_16, moe_w2_16 = moe_w1.astype(BF16), moe_w3.astype(BF16), moe_w2.astype(BF16)

    x2 = x.reshape(n, d)
    for l in range(depth):
        mod_l = mod[l]
        h = _norm(x2, norm1_g[l], mod_l, seq, 0, 1)
        proj = _matmul(h, w_in16, l, n_mix)
        y_a = _lru_branch(proj, bsz, seq, col["lru_x"], col["lru_gate"], lru_conv_w[l], lru_conv_b[l],
                          lru_wr[l], lru_br[l], lru_wi[l], lru_bi[l], lru_lambda[l])
        y_b = _pool_branch(proj, bsz, seq, col["pool"], pool_w[l], pool_b[l], pool_scale[l])
        qn, kt = _qk_prep(proj, bsz, seq, col["q"], col["k"], db, cos_t, sa_t, sb_t, seg,
                          dup(q_norm_g[l]), dup(k_norm_g[l]))
        lambda_init = 0.8 - 0.6 * math.exp(-0.3 * l)
        y_c = _flash(qn, kt, proj, bsz, seq, col["v"], lam_q1[l], lam_k1[l], lam_q2[l], lam_k2[l],
                     subln_g[l], lambda_init)
        y_d = _conv_branch(proj, bsz, seq, col["glu_val"], col["glu_gate"], cv_dw_w[l], cv_dw_b[l],
                           cv_ln_g[l], cv_ln_b[l])
        merged = _merge(h, (y_a, y_b, y_c, y_d), w_in16, l, n_mix, b_gate[l], w_branch16)
        x2 = _resid_matmul(merged, w_out16, l, x2, mod_l, seq, 2)

        w_router = jnp.concatenate([router_g_w[l], router_e_w[l]], axis=1)
        w_router = jnp.pad(w_router, ((0, 0), (0, LANES - w_router.shape[1]))).astype(BF16)
        b_router = jnp.pad(jnp.concatenate([router_g_b[l], router_e_b[l]]), (0, LANES - n_groups - n_exp)).reshape(1, LANES)
        h2, route = _router(x2, norm2_g[l], mod_l, seq, w_router, b_router, n_groups, per_group)
        tile_expert, n_tiles, src_token, pos_steps = _moe_plan(route, n_exp, ffn_tm, comb_tc)
        y_sorted = _ffn(h2, tile_expert, n_tiles, src_token, moe_w1_16, moe_w3_16, moe_w2_16, l, ffn_tm)
        x2 = _combine(pos_steps, y_sorted, x2, mod_l, route, seq, comb_tc)
    return x2.reshape(bsz, seq, d)
```

```python
import functools
import math

import jax
import jax.numpy as jnp
from jax import lax
from jax.experimental import pallas as pl
from jax.experimental.pallas import tpu as pltpu

EPS = 1e-6
LRU_C = 8.0
LRU_CONV_TAPS = 4
POOL_WINDOWS = (2, 4, 8, 16)
POOL_HALO = 16
CONV_HALO = 32
HEAD_DIM = 64
HEAD_W = 2 * HEAD_DIM
ROPE_DIM = HEAD_DIM // 4
ROPE_HALF = ROPE_DIM // 2
ROPE_THETA = 500000.0
TOP_K = 2
NEG_BIG = -1e30
LOG2_E = math.log2(math.e)
SUBLANES = 8
LANES = 128
VMEM_LIMIT_BYTES = 56 * 1024 * 1024

BF16 = jnp.bfloat16
F32 = jnp.float32


def _tile(n, pref):
    t = min(n, pref)
    assert n % t == 0, (n, pref)
    return t


def _params(sem, **kw):
    return pltpu.CompilerParams(dimension_semantics=sem, vmem_limit_bytes=VMEM_LIMIT_BYTES, **kw)


def _sigmoid(x):
    return 1.0 / (1.0 + jnp.exp(-x))


def _ada_kernel(c_ref, w_ref, b_ref, o_ref):
    c = c_ref[...]
    c_act = (c * _sigmoid(c)).astype(BF16)
    o_ref[...] = jnp.dot(c_act, w_ref[...].astype(BF16), preferred_element_type=F32) + b_ref[...]


def _ada_mod(c, ada_w, ada_b):
    depth, d, n_out = ada_w.shape
    bsz = c.shape[0]
    tn = _tile(n_out, 1024)
    return pl.pallas_call(
        _ada_kernel,
        name="ada_mod",
        grid=(depth, n_out // tn),
        in_specs=[
            pl.BlockSpec((bsz, d), lambda l, j: (0, 0)),
            pl.BlockSpec((None, d, tn), lambda l, j: (l, 0, j)),
            pl.BlockSpec((None, 1, tn), lambda l, j: (l, 0, j)),
        ],
        out_specs=pl.BlockSpec((None, bsz, tn), lambda l, j: (l, 0, j)),
        out_shape=jax.ShapeDtypeStruct((depth, bsz, n_out), F32),
        compiler_params=_params(("arbitrary", "arbitrary")),
    )(c, ada_w, ada_b.reshape(depth, 1, n_out))


def _modnorm(x, g, mod, shift_row, scale_row):
    y = x * lax.rsqrt(jnp.mean(x * x, axis=-1, keepdims=True) + EPS) * g
    return y * (1.0 + mod[scale_row:scale_row + 1, :]) + mod[shift_row:shift_row + 1, :]


def _norm_kernel(x_ref, g_ref, mod_ref, o_ref, *, shift_row, scale_row):
    o_ref[...] = _modnorm(x_ref[...], g_ref[...], mod_ref[...], shift_row, scale_row).astype(o_ref.dtype)


def _norm(x2, g, mod_l, seq, shift_row, scale_row):
    n, d = x2.shape
    tm = _tile(seq, 512)
    per_seq = seq // tm
    return pl.pallas_call(
        functools.partial(_norm_kernel, shift_row=shift_row, scale_row=scale_row),
        name="mod_norm",
        grid=(n // tm,),
        in_specs=[
            pl.BlockSpec((tm, d), lambda i: (i, 0)),
            pl.BlockSpec((1, d), lambda i: (0, 0)),
            pl.BlockSpec((None, 6, d), lambda i: (i // per_seq, 0, 0)),
        ],
        out_specs=pl.BlockSpec((tm, d), lambda i: (i, 0)),
        out_shape=jax.ShapeDtypeStruct((n, d), BF16),
        compiler_params=_params(("arbitrary",)),
    )(x2, g.reshape(1, d), mod_l)


def _matmul_kernel(a_ref, w_ref, o_ref):
    o_ref[...] = jnp.dot(a_ref[...], w_ref[...], preferred_element_type=F32).astype(o_ref.dtype)


def _matmul(a, w, layer, n_cols):
    m, k = a.shape
    tm = _tile(m, 1024)
    tn = _tile(n_cols, 1024)
    return pl.pallas_call(
        _matmul_kernel,
        name="in_proj",
        grid=(m // tm, n_cols // tn),
        in_specs=[
            pl.BlockSpec((tm, k), lambda i, j: (i, 0)),
            pl.BlockSpec((None, k, tn), lambda i, j: (layer, 0, j)),
        ],
        out_specs=pl.BlockSpec((tm, tn), lambda i, j: (i, j)),
        out_shape=jax.ShapeDtypeStruct((m, n_cols), BF16),
        compiler_params=_params(("arbitrary", "arbitrary")),
    )(a, w)


def _gelu_tanh(x):
    return 0.5 * x * (1.0 + jnp.tanh(math.sqrt(2.0 / math.pi) * (x + 0.044715 * (x * x * x))))


def _lru_kernel(x_ref, gate_ref, cw_ref, cb_ref, wr_ref, br_ref, wi_ref, bi_ref, lam_ref, o_ref,
                xe_ref, a_ref, u_ref, carry_ref):
    ts, db = x_ref.shape
    n_blocks, blk = wr_ref.shape[0], wr_ref.shape[1]
    halo = SUBLANES
    first = pl.program_id(1) == 0

    @pl.when(first)
    def _():
        xe_ref[0:halo, :] = jnp.zeros((halo, db), F32)
        carry_ref[...] = jnp.zeros_like(carry_ref)

    @pl.when(jnp.logical_not(first))
    def _():
        xe_ref[0:halo, :] = xe_ref[ts:ts + halo, :]

    xe_ref[halo:halo + ts, :] = x_ref[...].astype(F32)

    xa = jnp.zeros((ts, db), F32) + cb_ref[...]
    for j in range(LRU_CONV_TAPS):
        off = halo - (LRU_CONV_TAPS - 1) + j
        xa = xa + cw_ref[j:j + 1, :] * xe_ref[off:off + ts, :]

    xa16 = xa.astype(BF16)
    for hb in range(n_blocks):
        sl = slice(hb * blk, (hb + 1) * blk)
        a_ref[:, sl] = jnp.dot(xa16[:, sl], wr_ref[hb], preferred_element_type=F32)
        u_ref[:, sl] = jnp.dot(xa16[:, sl], wi_ref[hb], preferred_element_type=F32)
    r = _sigmoid(a_ref[...] + br_ref[...])
    ig = _sigmoid(u_ref[...] + bi_ref[...])
    z = -lam_ref[...]
    softplus = jnp.maximum(z, 0.0) + jnp.log(1.0 + jnp.exp(-jnp.abs(z)))
    a = jnp.exp((-LRU_C) * r * softplus)
    u = jnp.sqrt(1.0 - a * a) * (ig * xa)

    row = lax.broadcasted_iota(jnp.int32, (ts, db), 0) % SUBLANES
    for d in (1, 2, 4):
        keep = row >= d
        a_sh = jnp.where(keep, pltpu.roll(a, d, 0), 1.0)
        u_sh = jnp.where(keep, pltpu.roll(u, d, 0), 0.0)
        u = a * u_sh + u
        a = a * a_sh
    a_ref[...] = a
    u_ref[...] = u

    def body(c, h_prev):
        r0 = pl.multiple_of(c * SUBLANES, SUBLANES)
        h8 = u_ref[pl.ds(r0, SUBLANES), :] + a_ref[pl.ds(r0, SUBLANES), :] * h_prev
        u_ref[pl.ds(r0, SUBLANES), :] = h8
        return h8[SUBLANES - 1:SUBLANES, :]

    carry_ref[...] = lax.fori_loop(0, ts // SUBLANES, body, carry_ref[...])
    o_ref[...] = (_gelu_tanh(gate_ref[...].astype(F32)) * u_ref[...]).astype(o_ref.dtype)


def _lru_branch(proj, bsz, seq, col_x, col_gate, cw, cb, wr, br, wi, bi, lam):
    db = cw.shape[1]
    ts = _tile(seq, 256)
    per_seq = seq // ts
    n_blocks, blk = wr.shape[0], wr.shape[1]
    row = lambda v: v.reshape(1, db)
    vec = pl.BlockSpec((1, db), lambda b, i: (0, 0))
    wspec = pl.BlockSpec((n_blocks, blk, blk), lambda b, i: (0, 0, 0))
    return pl.pallas_call(
        _lru_kernel,
        name="lru_branch",
        grid=(bsz, per_seq),
        in_specs=[
            pl.BlockSpec((ts, db), lambda b, i: (b * per_seq + i, col_x)),
            pl.BlockSpec((ts, db), lambda b, i: (b * per_seq + i, col_gate)),
            pl.BlockSpec((LRU_CONV_TAPS, db), lambda b, i: (0, 0)),
            vec, wspec, vec, wspec, vec, vec,
        ],
        out_specs=pl.BlockSpec((ts, db), lambda b, i: (b * per_seq + i, 0)),
        out_shape=jax.ShapeDtypeStruct((bsz * seq, db), BF16),
        scratch_shapes=[
            pltpu.VMEM((ts + SUBLANES, db), F32),
            pltpu.VMEM((ts, db), F32),
            pltpu.VMEM((ts, db), F32),
            pltpu.VMEM((1, db), F32),
        ],
        compiler_params=_params(("arbitrary", "arbitrary")),
    )(proj, proj, cw, row(cb), wr.astype(BF16), row(br), wi.astype(BF16), row(bi), row(lam))


def _pool_kernel(x_ref, w_ref, b_ref, s_ref, o_ref, xe_ref):
    ts, db = x_ref.shape
    n_groups, pg = w_ref.shape[0], w_ref.shape[1]
    halo = POOL_HALO
    i = pl.program_id(1)

    @pl.when(i == 0)
    def _():
        xe_ref[0:halo, :] = jnp.zeros((halo, db), F32)

    @pl.when(i != 0)
    def _():
        xe_ref[0:halo, :] = xe_ref[ts:ts + halo, :]

    xe_ref[halo:halo + ts, :] = x_ref[...].astype(F32)
    t1 = (i * ts + 1 + lax.broadcasted_iota(jnp.int32, (ts, pg), 0)).astype(F32)
    for gi in range(n_groups):
        win = POOL_WINDOWS[gi]
        sl = slice(gi * pg, (gi + 1) * pg)
        x = xe_ref[halo:halo + ts, sl]
        acc = x
        for j in range(1, win):
            acc = acc + xe_ref[halo - j:halo - j + ts, sl]
        diff = acc / jnp.minimum(t1, float(win)) - x
        y = jnp.dot(diff.astype(BF16), w_ref[gi], preferred_element_type=F32) + b_ref[:, sl]
        o_ref[:, sl] = (y * s_ref[:, sl]).astype(o_ref.dtype)


def _pool_branch(proj, bsz, seq, col, w, b, scale):
    n_groups, pg = w.shape[0], w.shape[1]
    assert n_groups == len(POOL_WINDOWS)
    db = n_groups * pg
    ts = _tile(seq, 512)
    per_seq = seq // ts
    vec = pl.BlockSpec((1, db), lambda bb, i: (0, 0))
    return pl.pallas_call(
        _pool_kernel,
        name="pool_branch",
        grid=(bsz, per_seq),
        in_specs=[
            pl.BlockSpec((ts, db), lambda bb, i: (bb * per_seq + i, col)),
            pl.BlockSpec((n_groups, pg, pg), lambda bb, i: (0, 0, 0)),
            vec, vec,
        ],
        out_specs=pl.BlockSpec((ts, db), lambda bb, i: (bb * per_seq + i, 0)),
        out_shape=jax.ShapeDtypeStruct((bsz * seq, db), BF16),
        scratch_shapes=[pltpu.VMEM((ts + POOL_HALO, db), F32)],
        compiler_params=_params(("arbitrary", "arbitrary")),
    )(proj, w.astype(BF16), b.reshape(1, db), scale.reshape(1, db))


def _conv_kernel(val_ref, gate_ref, w_ref, b_ref, g_ref, beta_ref, o_ref, ue_ref, sh_ref, *, row_block):
    ts, db = val_ref.shape
    taps = w_ref.shape[0]
    halo = CONV_HALO
    i = pl.program_id(1)

    @pl.when(i == 0)
    def _():
        ue_ref[0:halo, :] = jnp.zeros((halo, db), F32)

    @pl.when(i != 0)
    def _():
        ue_ref[0:halo, :] = ue_ref[ts:ts + halo, :]

    ue_ref[halo:halo + ts, :] = val_ref[...].astype(F32) * _sigmoid(gate_ref[...].astype(F32))
    span = sh_ref.shape[1]
    for p in range(1, SUBLANES):
        sh_ref[p - 1] = ue_ref[p:p + span, :]
    base = halo - (taps - 1)
    for rb in range(ts // row_block):
        r0 = rb * row_block
        acc = jnp.zeros((row_block, db), F32) + b_ref[...]
        for j in range(taps):
            a, p = divmod(base + j, SUBLANES)
            row = r0 + a * SUBLANES
            window = ue_ref[row:row + row_block, :] if p == 0 else sh_ref[p - 1, row:row + row_block, :]
            acc = acc + w_ref[j:j + 1, :] * window
        mu = jnp.mean(acc, axis=-1, keepdims=True)
        cen = acc - mu
        var = jnp.mean(cen * cen, axis=-1, keepdims=True)
        y = cen * lax.rsqrt(var + EPS) * g_ref[...] + beta_ref[...]
        o_ref[r0:r0 + row_block, :] = (y * _sigmoid(y)).astype(o_ref.dtype)


def _conv_branch(proj, bsz, seq, col_val, col_gate, w, b, g, beta):
    taps, db = w.shape
    assert taps - 1 <= CONV_HALO
    ts = _tile(seq, 256)
    per_seq = seq // ts
    row = lambda v: v.reshape(1, db)
    vec = pl.BlockSpec((1, db), lambda bb, i: (0, 0))
    return pl.pallas_call(
        functools.partial(_conv_kernel, row_block=_tile(ts, 32)),
        name="conv_branch",
        grid=(bsz, per_seq),
        in_specs=[
            pl.BlockSpec((ts, db), lambda bb, i: (bb * per_seq + i, col_val)),
            pl.BlockSpec((ts, db), lambda bb, i: (bb * per_seq + i, col_gate)),
            pl.BlockSpec((taps, db), lambda bb, i: (0, 0)),
            vec, vec, vec,
        ],
        out_specs=pl.BlockSpec((ts, db), lambda bb, i: (bb * per_seq + i, 0)),
        out_shape=jax.ShapeDtypeStruct((bsz * seq, db), BF16),
        scratch_shapes=[
            pltpu.VMEM((ts + CONV_HALO, db), F32),
            pltpu.VMEM((SUBLANES - 1, ts + CONV_HALO - SUBLANES, db), F32),
        ],
        compiler_params=_params(("arbitrary", "arbitrary")),
    )(proj, proj, w, row(b), row(g), row(beta))


def _qk_prep_kernel(q_ref, k_ref, cos_ref, sa_ref, sb_ref, seg_ref, qg_ref, kg_ref, qo_ref, kt_ref, kn_ref):
    ts, db = q_ref.shape
    cos, sa, sb = cos_ref[...], sa_ref[...], sb_ref[...]

    def norm_rope(x, g):
        ms = jnp.dot((x * x).astype(BF16), seg_ref[...], preferred_element_type=F32)
        xn = x * lax.rsqrt(ms + EPS) * g
        return (xn * cos + pltpu.roll(xn, HEAD_W - ROPE_HALF, 1) * sa + pltpu.roll(xn, ROPE_HALF, 1) * sb)

    for h in range(db // HEAD_W):
        sl = slice(h * HEAD_W, (h + 1) * HEAD_W)
        q = norm_rope(q_ref[:, sl].astype(F32), qg_ref[...])
        qo_ref[:, sl] = (q * (LOG2_E / math.sqrt(HEAD_DIM))).astype(qo_ref.dtype)
        kn_ref[:, sl] = norm_rope(k_ref[:, sl].astype(F32), kg_ref[...])
    kt_ref[...] = kn_ref[...].T.astype(kt_ref.dtype)


def _qk_prep(proj, bsz, seq, col_q, col_k, db, cos_t, sa_t, sb_t, seg, qg, kg):
    ts = _tile(seq, 256)
    per_seq = seq // ts
    tab = pl.BlockSpec((ts, HEAD_W), lambda b, i: (b * per_seq + i, 0))
    vec = pl.BlockSpec((1, HEAD_W), lambda b, i: (0, 0))
    return pl.pallas_call(
        _qk_prep_kernel,
        name="qk_prep",
        grid=(bsz, per_seq),
        in_specs=[
            pl.BlockSpec((ts, db), lambda b, i: (b * per_seq + i, col_q)),
            pl.BlockSpec((ts, db), lambda b, i: (b * per_seq + i, col_k)),
            tab, tab, tab,
            pl.BlockSpec((HEAD_W, HEAD_W), lambda b, i: (0, 0)),
            vec, vec,
        ],
        out_specs=[
            pl.BlockSpec((ts, db), lambda b, i: (b * per_seq + i, 0)),
            pl.BlockSpec((None, db, ts), lambda b, i: (b, 0, i)),
        ],
        out_shape=[
            jax.ShapeDtypeStruct((bsz * seq, db), BF16),
            jax.ShapeDtypeStruct((bsz, db, seq), BF16),
        ],
        scratch_shapes=[pltpu.VMEM((ts, db), F32)],
        compiler_params=_params(("arbitrary", "arbitrary")),
    )(proj, proj, cos_t, sa_t, sb_t, seg, qg, kg)


def _flash_kernel(q_ref, kt_ref, v_ref, lq1_ref, lk1_ref, lq2_ref, lk2_ref, sg_ref, o_ref,
                  m_ref, l_ref, acc_ref, p_ref, s_ref, *, lambda_init, tk):
    tq = q_ref.shape[0]
    rows = 2 * tq
    i = pl.program_id(2)
    lane = lax.broadcasted_iota(jnp.int32, (tq, HEAD_W), 1)
    q = q_ref[...]
    zero = jnp.zeros_like(q)
    q2 = jnp.concatenate([jnp.where(lane < HEAD_DIM, q, zero), jnp.where(lane >= HEAD_DIM, q, zero)], axis=0)
    m_ref[...] = jnp.full(m_ref.shape, NEG_BIG, F32)
    l_ref[...] = jnp.zeros(l_ref.shape, F32)
    acc_ref[...] = jnp.zeros(acc_ref.shape, F32)
    p_ref[...] = jnp.zeros(p_ref.shape, p_ref.dtype)

    def pv(j):
        k0 = pl.multiple_of(j * tk, tk)
        return jnp.dot(p_ref[...], v_ref[pl.ds(k0, tk), :], preferred_element_type=F32)

    def scores(j):
        k0 = pl.multiple_of(j * tk, tk)
        return jnp.dot(q2, kt_ref[:, pl.ds(k0, tk)], preferred_element_type=F32)

    n_full = (i * tq) // tk
    n_all = ((i + 1) * tq + tk - 1) // tk
    s_ref[...] = scores(0)

    def step(j, masked):
        k0 = pl.multiple_of(j * tk, tk)
        s = s_ref[...]
        pv_prev = pv(jnp.maximum(j - 1, 0))
        s_ref[...] = scores(jnp.minimum(j + 1, n_all - 1))
        if masked:
            qpos = i * tq + lax.broadcasted_iota(jnp.int32, (rows, tk), 0) % tq
            kpos = k0 + lax.broadcasted_iota(jnp.int32, (rows, tk), 1)
            s = jnp.where(kpos <= qpos, s, NEG_BIG)
        cols = [s[:, cc * LANES:(cc + 1) * LANES] for cc in range(tk // LANES)]
        m_prev = m_ref[...]
        m_next = jnp.maximum(m_prev, jnp.max(functools.reduce(jnp.maximum, cols), axis=-1, keepdims=True))
        alpha = jnp.exp2(m_prev - m_next)
        ps = [jnp.exp2(col - m_next) for col in cols]
        l_ref[...] = alpha * l_ref[...] + functools.reduce(jnp.add, ps)
        acc_ref[...] = alpha * (acc_ref[...] + pv_prev)
        p_ref[...] = jnp.concatenate([x.astype(p_ref.dtype) for x in ps], axis=1)
        m_ref[...] = m_next

    def full_body(j, c):
        step(j, False)
        return c

    def diag_body(j, c):
        step(j, True)
        return c

    lax.fori_loop(0, n_full, full_body, 0)
    lax.fori_loop(n_full, n_all, diag_body, 0)

    lam = (jnp.exp(jnp.sum(lq1_ref[...] * lk1_ref[...], axis=-1, keepdims=True))
           - jnp.exp(jnp.sum(lq2_ref[...] * lk2_ref[...], axis=-1, keepdims=True)) + lambda_init)
    o_all = (acc_ref[...] + pv(n_all - 1)) / jnp.sum(l_ref[...], axis=-1, keepdims=True)
    o = o_all[0:tq, :] - lam * o_all[tq:rows, :]
    o = o * lax.rsqrt(jnp.mean(o * o, axis=-1, keepdims=True) + EPS) * sg_ref[...]
    o_ref[...] = (o * (1.0 - lambda_init)).astype(o_ref.dtype)


def _flash(qn, kt, proj, bsz, seq, col_v, lq1, lk1, lq2, lk2, sub_g, lambda_init):
    db = qn.shape[1]
    heads = db // HEAD_W
    tq = _tile(seq, 512)
    tk = _tile(seq, 512)
    per_seq = seq // tq
    vec64 =pl.BlockSpec((1, HEAD_DIM), lambda b, h, i: (0, 0))
    row64 = lambda v: v.reshape(1, HEAD_DIM)
    return pl.pallas_call(
        functools.partial(_flash_kernel, lambda_init=lambda_init, tk=tk),
        name="diff_attn",
        grid=(bsz, heads, per_seq),
        in_specs=[
            pl.BlockSpec((tq, HEAD_W), lambda b, h, i: (b * per_seq + i, h)),
            pl.BlockSpec((None, HEAD_W, seq), lambda b, h, i: (b, h, 0)),
            pl.BlockSpec((seq, HEAD_W), lambda b, h, i: (b, col_v * heads + h)),
            vec64, vec64, vec64, vec64,
            pl.BlockSpec((1, HEAD_W), lambda b, h, i: (0, 0)),
        ],
        out_specs=pl.BlockSpec((tq, HEAD_W), lambda b, h, i: (b * per_seq + i, h)),
        out_shape=jax.ShapeDtypeStruct((bsz * seq, db), BF16),
        scratch_shapes=[pltpu.VMEM((2 * tq, HEAD_W), F32)] * 3 + [
            pltpu.VMEM((2 * tq, tk), BF16),
            pltpu.VMEM((2 * tq, tk), F32),
        ],
        compiler_params=_params(("arbitrary", "arbitrary", "arbitrary")),
    )(qn, kt, proj, row64(lq1), row64(lk1), row64(lq2), row64(lk2), sub_g.reshape(1, HEAD_W))


def _rope_tables(positions):
    inv_freq = ROPE_THETA ** (-jnp.arange(0, ROPE_DIM, 2, dtype=F32) / ROPE_DIM)
    lane = jnp.arange(HEAD_W, dtype=jnp.int32) % HEAD_DIM
    freq = jnp.where(lane < ROPE_DIM, jnp.take(inv_freq, lane % ROPE_HALF), 0.0)
    ang = positions.astype(F32).reshape(-1, 1) * freq[None, :]
    sin = jnp.sin(ang)
    return jnp.cos(ang), jnp.where(lane < ROPE_HALF, -sin, 0.0), jnp.where(lane >= ROPE_HALF, sin, 0.0)


def _merge_kernel(h_ref, ya_ref, yb_ref, yc_ref, yd_ref, wa_ref, wb_ref, wc_ref, wd_ref,
                  bg_ref, wbr_ref, o_ref):
    h = h_ref[...]
    acc = None
    for bi, (y_ref, wg_ref) in enumerate(((ya_ref, wa_ref), (yb_ref, wb_ref), (yc_ref, wc_ref), (yd_ref, wd_ref))):
        gate = _sigmoid(jnp.dot(h, wg_ref[...], preferred_element_type=F32) + bg_ref[bi:bi + 1, :])
        term = gate * jnp.dot(y_ref[...], wbr_ref[bi], preferred_element_type=F32)
        acc = term if acc is None else acc + term
    o_ref[...] = acc.astype(o_ref.dtype)


def _merge(h, ys, w_in16, layer, n_mix, b_gate, w_branch16):
    n, d = h.shape
    db = ys[0].shape[1]
    n_branch = len(ys)
    tm = _tile(n, 1024)
    tn = _tile(d, 256)
    gate_blk = [(n_mix + b * d) // tn for b in range(n_branch)]
    once = pl.Buffered(1)
    wspecs = [pl.BlockSpec((None, d, tn), functools.partial(lambda j, i, base: (layer, 0, base + j), base=gb),
                           pipeline_mode=once) for gb in gate_blk]
    yspec = pl.BlockSpec((tm, db), lambda j, i: (i, 0))
    return pl.pallas_call(
        _merge_kernel,
        name="gated_merge",
        grid=(d // tn, n // tm),
        in_specs=[pl.BlockSpec((tm, d), lambda j, i: (i, 0))] + [yspec] * n_branch + wspecs + [
            pl.BlockSpec((n_branch, tn), lambda j, i: (0, j)),
            pl.BlockSpec((None, n_branch, db, tn), lambda j, i: (layer, 0, 0, j), pipeline_mode=once),
        ],
        out_specs=pl.BlockSpec((tm, tn), lambda j, i: (i, j)),
        out_shape=jax.ShapeDtypeStruct((n, d), BF16),
        compiler_params=_params(("arbitrary", "arbitrary")),
    )(h, *ys, *([w_in16] * n_branch), b_gate, w_branch16)


def _resid_matmul_kernel(a_ref, w_ref, x_ref, mod_ref, o_ref, *, gate_row):
    y = jnp.dot(a_ref[...], w_ref[...], preferred_element_type=F32)
    o_ref[...] = x_ref[...] + mod_ref[gate_row:gate_row + 1, :] * y


def _resid_matmul(a, w16, layer, x2, mod_l, seq, gate_row):
    n, k = a.shape
    d = w16.shape[2]
    tm = _tile(seq, 1024)
    tn = _tile(d, 1024)
    per_seq = seq // tm
    return pl.pallas_call(
        functools.partial(_resid_matmul_kernel, gate_row=gate_row),
        name="out_proj_resid",
        grid=(n // tm, d // tn),
        in_specs=[
            pl.BlockSpec((tm, k), lambda i, j: (i, 0)),
            pl.BlockSpec((None, k, tn), lambda i, j: (layer, 0, j)),
            pl.BlockSpec((tm, tn), lambda i, j: (i, j)),
            pl.BlockSpec((None, 6, tn), lambda i, j: (i // per_seq, 0, j)),
        ],
        out_specs=pl.BlockSpec((tm, tn), lambda i, j: (i, j)),
        out_shape=jax.ShapeDtypeStruct((n, d), F32),
        compiler_params=_params(("arbitrary", "arbitrary")),
    )(a, w16, x2, mod_l)


def _pack_halves(v):
    w = v.shape[1] // 2
    bits = lambda t: pltpu.bitcast(t.astype(BF16).astype(F32), jnp.uint32)
    return bits(v[:, w:]) | (bits(v[:, :w]) >> 16)


def _unpack_halves(p):
    return (pltpu.bitcast(p << 16, F32), pltpu.bitcast(p & jnp.uint32(0xFFFF0000), F32))


def _store_tiled_rows(ref3, val2):
    for s in range(ref3.shape[0]):
        ref3[s] = val2[:, s * LANES:(s + 1) * LANES]


def _load_tiled_rows(ref3):
    return jnp.concatenate([ref3[s] for s in range(ref3.shape[0])], axis=1)


def _router_kernel(x_ref, g_ref, mod_ref, wr_ref, br_ref, h_ref, r_ref, *, n_groups, per_group):
    h2 = _modnorm(x_ref[...], g_ref[...], mod_ref[...], 3, 4)
    _store_tiled_rows(h_ref, _pack_halves(h2))
    logits = jnp.dot(h2.astype(BF16), wr_ref[...], preferred_element_type=F32) + br_ref[...]
    lane = lax.broadcasted_iota(jnp.int32, logits.shape, 1)
    n_exp = n_groups * per_group
    is_g = lane < n_groups
    gl = jnp.where(is_g, logits, NEG_BIG)
    gmax = jnp.max(gl, axis=-1, keepdims=True)
    g_idx = jnp.min(jnp.where(gl == gmax, lane, LANES), axis=-1, keepdims=True)
    g_top = 1.0 / jnp.sum(jnp.where(is_g, jnp.exp(gl - gmax), 0.0), axis=-1, keepdims=True)
    lo = n_groups + g_idx * per_group
    in_grp = jnp.logical_and(lane >= lo, lane < lo + per_group)
    el = jnp.where(in_grp, logits, NEG_BIG)
    v1 = jnp.max(el, axis=-1, keepdims=True)
    i1 = jnp.min(jnp.where(el == v1, lane, LANES), axis=-1, keepdims=True)
    el2 = jnp.where(lane == i1, NEG_BIG, el)
    v2 = jnp.max(el2, axis=-1, keepdims=True)
    i2 = jnp.min(jnp.where(el2 == v2, lane, LANES), axis=-1, keepdims=True)
    e = jnp.exp(v2 - v1)
    w1 = g_top / (1.0 + e)
    w2 = g_top * e / (1.0 + e)
    del n_exp
    out = jnp.where(lane == 0, (i1 - n_groups).astype(F32),
                    jnp.where(lane == 1, (i2 - n_groups).astype(F32),
                              jnp.where(lane == 2, w1, jnp.where(lane == 3, w2, 0.0))))
    r_ref[...] = out


def _router(x2, g, mod_l, seq, w_router16, b_router, n_groups, per_group):
    n, d = x2.shape
    tm = _tile(seq, 256)
    per_seq = seq // tm
    return pl.pallas_call(
        functools.partial(_router_kernel, n_groups=n_groups, per_group=per_group),
        name="moe_router",
        grid=(n // tm,),
        in_specs=[
            pl.BlockSpec((tm, d), lambda i: (i, 0)),
            pl.BlockSpec((1, d), lambda i: (0, 0)),
            pl.BlockSpec((None, 6, d), lambda i: (i // per_seq, 0, 0)),
            pl.BlockSpec((d, LANES), lambda i: (0, 0)),
            pl.BlockSpec((1, LANES), lambda i: (0, 0)),
        ],
        out_specs=[
            pl.BlockSpec((d // (2 * LANES), tm, LANES), lambda i: (0, i, 0)),
            pl.BlockSpec((tm, LANES), lambda i: (i, 0)),
        ],
        out_shape=[
            jax.ShapeDtypeStruct((d // (2 * LANES), n, LANES), jnp.uint32),
            jax.ShapeDtypeStruct((n, LANES), F32),
        ],
        compiler_params=_params(("arbitrary",)),
    )(x2, g.reshape(1, d), mod_l, w_router16, b_router)


GATHER_UNROLL = 8
FFN_GATHER_SLOTS = 3


def _start_row_gather(idx_ref, base, n_rows, src_hbm, dst, sem):
    def body(g, c):
        r0 = pl.multiple_of(g * GATHER_UNROLL, GATHER_UNROLL)
        for k in range(GATHER_UNROLL):
            row = idx_ref[base + r0 + k]
            pltpu.make_async_copy(src_hbm.at[:, pl.ds(row, 1), :], dst.at[:, pl.ds(r0 + k, 1), :],
                                  sem).start(priority=k % 2)
        return c
    lax.fori_loop(0, n_rows // GATHER_UNROLL, body, 0)


def _wait_row_gather(src_hbm, dst, sem):
    pltpu.make_async_copy(src_hbm.at[:, pl.ds(0, dst.shape[1]), :], dst, sem).wait()


def _ffn_kernel(tile_exp_ref, n_tiles_ref, ring_ref, src_ref, h_hbm, w1_hbm, w3_hbm, w2_hbm, y_ref,
                xbuf, sem, w1buf, w3buf, w2buf, wsem, *, tm, layer):
    t = pl.program_id(0)
    n_tiles = n_tiles_ref[0]
    max_tiles = pl.num_programs(0)
    n_slots = xbuf.shape[0]
    ahead = n_slots - 1
    slot = t % n_slots

    def start_tile(tile):
        s = tile % n_slots
        _start_row_gather(src_ref, tile * tm, tm, h_hbm, xbuf.at[s], sem.at[s])

    for first in range(ahead):
        @pl.when(jnp.logical_and(t == 0, first < n_tiles))
        def _():
            start_tile(first)

    @pl.when(t + ahead < n_tiles)
    def _():
        start_tile(t + ahead)

    def weight_copies(expert, s):
        return [pltpu.make_async_copy(src.at[layer, expert], dst.at[s], wsem.at[s])
                for src, dst in ((w1_hbm, w1buf), (w3_hbm, w3buf), (w2_hbm, w2buf))]

    @pl.when(t < n_tiles)
    def _():
        ws = ring_ref[max_tiles + t]

        @pl.when(ring_ref[t] == 1)
        def _():
            @pl.when(t == 0)
            def _():
                for c in weight_copies(tile_exp_ref[0], ws):
                    c.start()

            next_expert = ring_ref[2 * max_tiles + t]

            @pl.when(next_expert >= 0)
            def _():
                for c in weight_copies(next_expert, 1 - ws):
                    c.start()

            for c in weight_copies(tile_exp_ref[t], ws):
                c.wait()

        _wait_row_gather(h_hbm, xbuf.at[slot], sem.at[slot])
        x_lo, x_hi = (v.astype(BF16) for v in _unpack_halves(_load_tiled_rows(xbuf.at[slot])))
        half = x_lo.shape[1]

        def up_proj(wbuf):
            return (jnp.dot(x_lo, wbuf[ws, 0:half, :], preferred_element_type=F32)
                    + jnp.dot(x_hi, wbuf[ws, half:2 * half, :], preferred_element_type=F32))

        a = up_proj(w1buf)
        b = up_proj(w3buf)
        hid = (a * _sigmoid(a) * b).astype(BF16)
        _store_tiled_rows(y_ref, _pack_halves(jnp.dot(hid, w2buf[ws], preferred_element_type=F32)))

    @pl.when(t >= n_tiles)
    def _():
        y_ref[...] = jnp.zeros_like(y_ref)


def _ffn(h2, tile_expert, n_tiles, ring, src_token, w1, w3, w2, layer, tm):
    d, de = w1.shape[2], w1.shape[3]
    n_chunks = h2.shape[0]
    max_tiles = tile_expert.shape[0]
    any_space = pl.BlockSpec(memory_space=pl.ANY)
    grid_spec = pltpu.PrefetchScalarGridSpec(
        num_scalar_prefetch=4,
        grid=(max_tiles,),
        in_specs=[any_space] * 4,
        out_specs=pl.BlockSpec((n_chunks, tm, LANES), lambda t, te, nt, rg, src: (0, t, 0)),
        scratch_shapes=[
            pltpu.VMEM((FFN_GATHER_SLOTS, n_chunks, tm, LANES), jnp.uint32),
            pltpu.SemaphoreType.DMA((FFN_GATHER_SLOTS,)),
            pltpu.VMEM((2, d, de), BF16),
            pltpu.VMEM((2, d, de), BF16),
            pltpu.VMEM((2, de, d), BF16),
            pltpu.SemaphoreType.DMA((2,)),
        ],
    )
    return pl.pallas_call(
        functools.partial(_ffn_kernel, tm=tm, layer=layer),
        name="moe_ffn",
        grid_spec=grid_spec,
        out_shape=jax.ShapeDtypeStruct((n_chunks, max_tiles * tm, LANES), jnp.uint32),
        compiler_params=_params(("arbitrary",), disable_bounds_checks=True),
    )(tile_expert, n_tiles, ring, src_token, h2, w1, w3, w2)


def _combine_kernel(pos_ref, y_hbm, x_ref, mod_ref, w_ref, o_ref, ybuf, sem, *, tc):
    t = pl.program_id(0)
    n_steps = pl.num_programs(0)
    slot = t % 2

    def start_step(step, s):
        _start_row_gather(pos_ref, step * (TOP_K * tc), TOP_K * tc, y_hbm, ybuf.at[s], sem.at[s])

    @pl.when(t == 0)
    def _():
        start_step(0, 0)

    @pl.when(t + 1 < n_steps)
    def _():
        start_step(t + 1, 1 - slot)

    _wait_row_gather(y_hbm, ybuf.at[slot], sem.at[slot])
    half = ybuf.shape[1] * ybuf.shape[3]
    chunk = _tile(tc, 2 * SUBLANES)
    for r0 in range(0, tc, chunk):
        w = w_ref[r0:r0 + chunk, :]
        first = _unpack_halves(_load_tiled_rows(ybuf.at[slot, :, r0:r0 + chunk, :]))
        second = _unpack_halves(_load_tiled_rows(ybuf.at[slot, :, tc + r0:tc + r0 + chunk, :]))
        for part in range(2):
            cols = slice(part * half, (part + 1) * half)
            moe = w[:, 2:3] * first[part] + w[:, 3:4] * second[part]
            o_ref[r0:r0 + chunk, cols] = x_ref[r0:r0 + chunk, cols] + mod_ref[5:6, cols] * moe


def _combine(pos_steps, y_sorted, x2, mod_l, route, seq, tc):
    n, d = x2.shape
    per_seq = seq // tc
    grid_spec = pltpu.PrefetchScalarGridSpec(
        num_scalar_prefetch=1,
        grid=(n // tc,),
        in_specs=[
            pl.BlockSpec(memory_space=pl.ANY),
            pl.BlockSpec((tc, d), lambda t, pos: (t, 0)),
            pl.BlockSpec((None, 6, d), lambda t, pos: (t // per_seq, 0, 0)),
            pl.BlockSpec((tc, LANES), lambda t, pos: (t, 0)),
        ],
        out_specs=pl.BlockSpec((tc, d), lambda t, pos: (t, 0)),
        scratch_shapes=[
            pltpu.VMEM((2, y_sorted.shape[0], TOP_K * tc, LANES), jnp.uint32),
            pltpu.SemaphoreType.DMA((2,)),
        ],
    )
    return pl.pallas_call(
        functools.partial(_combine_kernel, tc=tc),
        name="moe_combine",
        grid_spec=grid_spec,
        out_shape=jax.ShapeDtypeStruct((n, d), F32),
        compiler_params=_params(("arbitrary",), disable_bounds_checks=True),
    )(pos_steps, y_sorted, x2, mod_l, route)


def _moe_plan(route, n_exp, tm, tc):
    n = route.shape[0]
    eid = route[:, 0:TOP_K].astype(jnp.int32)
    flat_e = eid.reshape(-1)
    onehot = (flat_e[:, None] == jnp.arange(n_exp, dtype=jnp.int32)[None, :]).astype(jnp.int32)
    csum = jnp.cumsum(onehot, axis=0)
    rank = jnp.sum(csum * onehot, axis=1) - 1
    counts = csum[-1]
    tiles = (counts + tm - 1) // tm
    tile_end = jnp.cumsum(tiles)
    tile_start = tile_end - tiles
    n_tiles = tile_end[-1]
    pos = jnp.sum(onehot * (tile_start * tm)[None, :], axis=1) + rank
    max_tiles = (TOP_K * n) // tm + n_exp
    token = jnp.arange(TOP_K * n, dtype=jnp.int32) // TOP_K
    src_token = jnp.zeros((max_tiles * tm,), jnp.int32).at[pos].set(token)
    tix = jnp.minimum(jnp.arange(max_tiles, dtype=jnp.int32), n_tiles - 1)
    tile_expert = jnp.sum((tile_end[None, :] <= tix[:, None]).astype(jnp.int32), axis=1)
    t_idx = jnp.arange(max_tiles, dtype=jnp.int32)
    prev_expert = jnp.concatenate([jnp.full((1,), -1, jnp.int32), tile_expert[:-1]])
    first = jnp.logical_and(t_idx < n_tiles, tile_expert != prev_expert)
    w_slot = (jnp.cumsum(first.astype(jnp.int32)) - 1) % 2
    first_at_or_after = lax.cummin(jnp.where(first, t_idx, max_tiles), axis=0, reverse=True)
    next_first = jnp.concatenate([first_at_or_after[1:], jnp.full((1,), max_tiles, jnp.int32)])
    next_expert = jnp.where(next_first < max_tiles, tile_expert[jnp.minimum(next_first, max_tiles - 1)], -1)
    ring = jnp.stack([first.astype(jnp.int32), w_slot.astype(jnp.int32), next_expert.astype(jnp.int32)])
    pos_steps = pos.reshape(n // tc, tc, TOP_K).transpose(0, 2, 1).reshape(-1)
    return (tile_expert, n_tiles.reshape(1).astype(jnp.int32), ring.reshape(-1), src_token,
            pos_steps.astype(jnp.int32))


def kernel(x, c, positions, ada_w, ada_b, norm1_g, norm2_g, w_in, b_gate, lru_conv_w, lru_conv_b, lru_wr, lru_br, lru_wi, lru_bi, lru_lambda, pool_w, pool_b, pool_scale, q_norm_g, k_norm_g, lam_q1, lam_k1, lam_q2, lam_k2, subln_g, cv_dw_w, cv_dw_b, cv_ln_g, cv_ln_b, w_branch, w_out, router_g_w, router_g_b, router_e_w, router_e_b, moe_w1, moe_w3, moe_w2):
    bsz, seq, d = x.shape
    depth = ada_w.shape[0]
    db = lru_conv_w.shape[2]
    n_branch = w_branch.shape[1]
    n_mix = w_in.shape[2] - n_branch * d
    assert n_mix == 8 * db and q_norm_g.shape[1] == HEAD_DIM
    n_groups = router_g_w.shape[2]
    n_exp = router_e_w.shape[2]
    per_group = n_exp // n_groups
    assert n_groups + n_exp <= LANES
    n = bsz * seq
    col = {name: idx for idx, name in enumerate(("lru_x", "lru_gate", "pool", "q", "k", "v", "glu_val", "glu_gate"))}

    mod = _ada_mod(c, ada_w, ada_b).reshape(depth, bsz, 6, d)
    cos_t, sa_t, sb_t = _rope_tables(positions)
    seg = jnp.kron(jnp.eye(HEAD_W // HEAD_DIM, dtype=F32), jnp.full((HEAD_DIM, HEAD_DIM), 1.0 / HEAD_DIM, F32)).astype(BF16)
    dup = lambda g: jnp.concatenate([g, g]).reshape(1, HEAD_W)
    ffn_tm = _tile(TOP_K * n, 256)
    comb_tc = _tile(seq, 256)

    w_in16, w_branch16, w_out16 = w_in.astype(BF16), w_branch.astype(BF16), w_out.astype(BF16)
    moe_w1_16, moe_w3_16, moe_w2_16 = moe_w1.astype(BF16), moe_w3.astype(BF16), moe_w2.astype(BF16)

    x2 = x.reshape(n, d)
    for l in range(depth):
        mod_l = mod[l]
        h = _norm(x2, norm1_g[l], mod_l, seq, 0, 1)
        proj = _matmul(h, w_in16, l, n_mix)
        y_a = _lru_branch(proj, bsz, seq, col["lru_x"], col["lru_gate"], lru_conv_w[l], lru_conv_b[l],
                          lru_wr[l], lru_br[l], lru_wi[l], lru_bi[l], lru_lambda[l])
        y_b = _pool_branch(proj, bsz, seq, col["pool"], pool_w[l], pool_b[l], pool_scale[l])
        qn, kt = _qk_prep(proj, bsz, seq, col["q"], col["k"], db, cos_t, sa_t, sb_t, seg,
                          dup(q_norm_g[l]), dup(k_norm_g[l]))
        lambda_init = 0.8 - 0.6 * math.exp(-0.3 * l)
        y_c = _flash(qn, kt, proj, bsz, seq, col["v"], lam_q1[l], lam_k1[l], lam_q2[l], lam_k2[l],
                     subln_g[l], lambda_init)
        y_d = _conv_branch(proj, bsz, seq, col["glu_val"], col["glu_gate"], cv_dw_w[l], cv_dw_b[l],
                           cv_ln_g[l], cv_ln_b[l])
        merged = _merge(h, (y_a, y_b, y_c, y_d), w_in16, l, n_mix, b_gate[l], w_branch16)
        x2 = _resid_matmul(merged, w_out16, l, x2, mod_l, seq, 2)

        w_router = jnp.concatenate([router_g_w[l], router_e_w[l]], axis=1)
        w_router = jnp.pad(w_router, ((0, 0), (0, LANES - w_router.shape[1]))).astype(BF16)
        b_router = jnp.pad(jnp.concatenate([router_g_b[l], router_e_b[l]]), (0, LANES - n_groups - n_exp)).reshape(1, LANES)
        h2, route = _router(x2, norm2_g[l], mod_l, seq, w_router, b_router, n_groups, per_group)
        tile_expert, n_tiles, ring, src_token, pos_steps = _moe_plan(route, n_exp, ffn_tm, comb_tc)
        y_sorted = _ffn(h2, tile_expert, n_tiles, ring, src_token, moe_w1_16, moe_w3_16, moe_w2_16, l, ffn_tm)
        x2 = _combine(pos_steps, y_sorted, x2, mod_l, route, seq, comb_tc)
    return x2.reshape(bsz, seq, d)
```

```python
import functools
import math

import jax
import jax.numpy as jnp
from jax import lax
from jax.experimental import pallas as pl
from jax.experimental.pallas import tpu as pltpu

EPS = 1e-6
LRU_C = 8.0
LRU_CONV_TAPS = 4
POOL_WINDOWS = (2, 4, 8, 16)
POOL_HALO = 16
CONV_HALO = 32
HEAD_DIM = 64
HEAD_W = 2 * HEAD_DIM
ROPE_DIM = HEAD_DIM // 4
ROPE_HALF = ROPE_DIM // 2
ROPE_THETA = 500000.0
TOP_K = 2
NEG_BIG = -1e30
LOG2_E = math.log2(math.e)
SUBLANES = 8
LANES = 128
VMEM_LIMIT_BYTES = 56 * 1024 * 1024

BF16 = jnp.bfloat16
F32 = jnp.float32


def _tile(n, pref):
    t = min(n, pref)
    assert n % t == 0, (n, pref)
    return t


def _params(sem, **kw):
    return pltpu.CompilerParams(dimension_semantics=sem, vmem_limit_bytes=VMEM_LIMIT_BYTES, **kw)


def _sigmoid(x):
    return 1.0 / (1.0 + jnp.exp(-x))


def _ada_kernel(c_ref, w_ref, b_ref, o_ref):
    c = c_ref[...]
    c_act = (c * _sigmoid(c)).astype(BF16)
    o_ref[...] = jnp.dot(c_act, w_ref[...].astype(BF16), preferred_element_type=F32) + b_ref[...]


def _ada_mod(c, ada_w, ada_b):
    depth, d, n_out = ada_w.shape
    bsz = c.shape[0]
    tn = _tile(n_out, 1024)
    return pl.pallas_call(
        _ada_kernel,
        name="ada_mod",
        grid=(depth, n_out // tn),
        in_specs=[
            pl.BlockSpec((bsz, d), lambda l, j: (0, 0)),
            pl.BlockSpec((None, d, tn), lambda l, j: (l, 0, j)),
            pl.BlockSpec((None, 1, tn), lambda l, j: (l, 0, j)),
        ],
        out_specs=pl.BlockSpec((None, bsz, tn), lambda l, j: (l, 0, j)),
        out_shape=jax.ShapeDtypeStruct((depth, bsz, n_out), F32),
        compiler_params=_params(("arbitrary", "arbitrary")),
    )(c, ada_w, ada_b.reshape(depth, 1, n_out))


def _modnorm(x, g, mod, shift_row, scale_row):
    y = x * lax.rsqrt(jnp.mean(x * x, axis=-1, keepdims=True) + EPS) * g
    return y * (1.0 + mod[scale_row:scale_row + 1, :]) + mod[shift_row:shift_row + 1, :]


def _norm_kernel(x_ref, g_ref, mod_ref, o_ref, *, shift_row, scale_row):
    o_ref[...] = _modnorm(x_ref[...], g_ref[...], mod_ref[...], shift_row, scale_row).astype(o_ref.dtype)


def _norm(x2, g, mod_l, seq, shift_row, scale_row):
    n, d = x2.shape
    tm = _tile(seq, 512)
    per_seq = seq // tm
    return pl.pallas_call(
        functools.partial(_norm_kernel, shift_row=shift_row, scale_row=scale_row),
        name="mod_norm",
        grid=(n // tm,),
        in_specs=[
            pl.BlockSpec((tm, d), lambda i: (i, 0)),
            pl.BlockSpec((1, d), lambda i: (0, 0)),
            pl.BlockSpec((None, 6, d), lambda i: (i // per_seq, 0, 0)),
        ],
        out_specs=pl.BlockSpec((tm, d), lambda i: (i, 0)),
        out_shape=jax.ShapeDtypeStruct((n, d), BF16),
        compiler_params=_params(("arbitrary",)),
    )(x2, g.reshape(1, d), mod_l)


def _matmul_kernel(a_ref, w_ref, o_ref):
    o_ref[...] = jnp.dot(a_ref[...], w_ref[...], preferred_element_type=F32).astype(o_ref.dtype)


def _matmul(a, w, layer, n_cols):
    m, k = a.shape
    tm = _tile(m, 1024)
    tn = _tile(n_cols, 1024)
    return pl.pallas_call(
        _matmul_kernel,
        name="in_proj",
        grid=(m // tm, n_cols // tn),
        in_specs=[
            pl.BlockSpec((tm, k), lambda i, j: (i, 0)),
            pl.BlockSpec((None, k, tn), lambda i, j: (layer, 0, j)),
        ],
        out_specs=pl.BlockSpec((tm, tn), lambda i, j: (i, j)),
        out_shape=jax.ShapeDtypeStruct((m, n_cols), BF16),
        compiler_params=_params(("arbitrary", "arbitrary")),
    )(a, w)


def _gelu_tanh(x):
    return 0.5 * x * (1.0 + jnp.tanh(math.sqrt(2.0 / math.pi) * (x + 0.044715 * (x * x * x))))


def _lru_kernel(x_ref, gate_ref, cw_ref, cb_ref, wr_ref, br_ref, wi_ref, bi_ref, lam_ref, o_ref,
                xe_ref, a_ref, u_ref, carry_ref):
    ts, db = x_ref.shape
    n_blocks, blk = wr_ref.shape[0], wr_ref.shape[1]
    halo = SUBLANES
    first = pl.program_id(1) == 0

    @pl.when(first)
    def _():
        xe_ref[0:halo, :] = jnp.zeros((halo, db), F32)
        carry_ref[...] = jnp.zeros_like(carry_ref)

    @pl.when(jnp.logical_not(first))
    def _():
        xe_ref[0:halo, :] = xe_ref[ts:ts + halo, :]

    xe_ref[halo:halo + ts, :] = x_ref[...].astype(F32)

    xa = jnp.zeros((ts, db), F32) + cb_ref[...]
    for j in range(LRU_CONV_TAPS):
        off = halo - (LRU_CONV_TAPS - 1) + j
        xa = xa + cw_ref[j:j + 1, :] * xe_ref[off:off + ts, :]

    xa16 = xa.astype(BF16)
    for hb in range(n_blocks):
        sl = slice(hb * blk, (hb + 1) * blk)
        a_ref[:, sl] = jnp.dot(xa16[:, sl], wr_ref[hb], preferred_element_type=F32)
        u_ref[:, sl] = jnp.dot(xa16[:, sl], wi_ref[hb], preferred_element_type=F32)
    r = _sigmoid(a_ref[...] + br_ref[...])
    ig = _sigmoid(u_ref[...] + bi_ref[...])
    z = -lam_ref[...]
    softplus = jnp.maximum(z, 0.0) + jnp.log(1.0 + jnp.exp(-jnp.abs(z)))
    a = jnp.exp((-LRU_C) * r * softplus)
    u = jnp.sqrt(1.0 - a * a) * (ig * xa)

    a = a.reshape(ts // SUBLANES, SUBLANES, db)
    u = u.reshape(ts // SUBLANES, SUBLANES, db)
    row = lax.broadcasted_iota(jnp.int32, a.shape, 1)
    for d in (1, 2, 4):
        keep = row >= d
        a_sh = jnp.where(keep, pltpu.roll(a, d, 1), 1.0)
        u_sh = jnp.where(keep, pltpu.roll(u, d, 1), 0.0)
        u = a * u_sh + u
        a = a * a_sh
    a_ref[...] = a.reshape(ts, db)
    u_ref[...] = u.reshape(ts, db)

    def body(c, h_prev):
        r0 = pl.multiple_of(c * SUBLANES, SUBLANES)
        h8 = u_ref[pl.ds(r0, SUBLANES), :] + a_ref[pl.ds(r0, SUBLANES), :] * h_prev
        u_ref[pl.ds(r0, SUBLANES), :] = h8
        return h8[SUBLANES - 1:SUBLANES, :]

    carry_ref[...] = lax.fori_loop(0, ts // SUBLANES, body, carry_ref[...])
    o_ref[...] = (_gelu_tanh(gate_ref[...].astype(F32)) * u_ref[...]).astype(o_ref.dtype)


def _lru_branch(proj, bsz, seq, col_x, col_gate, cw, cb, wr, br, wi, bi, lam):
    db = cw.shape[1]
    ts = _tile(seq, 256)
    per_seq = seq // ts
    n_blocks, blk = wr.shape[0], wr.shape[1]
    row = lambda v: v.reshape(1, db)
    vec = pl.BlockSpec((1, db), lambda b, i: (0, 0))
    wspec = pl.BlockSpec((n_blocks, blk, blk), lambda b, i: (0, 0, 0))
    return pl.pallas_call(
        _lru_kernel,
        name="lru_branch",
        grid=(bsz, per_seq),
        in_specs=[
            pl.BlockSpec((ts, db), lambda b, i: (b * per_seq + i, col_x)),
            pl.BlockSpec((ts, db), lambda b, i: (b * per_seq + i, col_gate)),
            pl.BlockSpec((LRU_CONV_TAPS, db), lambda b, i: (0, 0)),
            vec, wspec, vec, wspec, vec, vec,
        ],
        out_specs=pl.BlockSpec((ts, db), lambda b, i: (b * per_seq + i, 0)),
        out_shape=jax.ShapeDtypeStruct((bsz * seq, db), BF16),
        scratch_shapes=[
            pltpu.VMEM((ts + SUBLANES, db), F32),
            pltpu.VMEM((ts, db), F32),
            pltpu.VMEM((ts, db), F32),
            pltpu.VMEM((1, db), F32),
        ],
        compiler_params=_params(("arbitrary", "arbitrary")),
    )(proj, proj, cw, row(cb), wr.astype(BF16), row(br), wi.astype(BF16), row(bi), row(lam))


def _pool_kernel(x_ref, w_ref, b_ref, s_ref, o_ref, xe_ref):
    ts, db = x_ref.shape
    n_groups, pg = w_ref.shape[0], w_ref.shape[1]
    halo = POOL_HALO
    i = pl.program_id(1)

    @pl.when(i == 0)
    def _():
        xe_ref[0:halo, :] = jnp.zeros((halo, db), F32)

    @pl.when(i != 0)
    def _():
        xe_ref[0:halo, :] = xe_ref[ts:ts + halo, :]

    xe_ref[halo:halo + ts, :] = x_ref[...].astype(F32)
    t1 = (i * ts + 1 + lax.broadcasted_iota(jnp.int32, (ts, pg), 0)).astype(F32)
    for gi in range(n_groups):
        win = POOL_WINDOWS[gi]
        sl = slice(gi * pg, (gi + 1) * pg)
        x = xe_ref[halo:halo + ts, sl]
        acc = x
        for j in range(1, win):
            acc = acc + xe_ref[halo - j:halo - j + ts, sl]
        diff = acc / jnp.minimum(t1, float(win)) - x
        y = jnp.dot(diff.astype(BF16), w_ref[gi], preferred_element_type=F32) + b_ref[:, sl]
        o_ref[:, sl] = (y * s_ref[:, sl]).astype(o_ref.dtype)


def _pool_branch(proj, bsz, seq, col, w, b, scale):
    n_groups, pg = w.shape[0], w.shape[1]
    assert n_groups == len(POOL_WINDOWS)
    db = n_groups * pg
    ts = _tile(seq, 512)
    per_seq = seq // ts
    vec = pl.BlockSpec((1, db), lambda bb, i: (0, 0))
    return pl.pallas_call(
        _pool_kernel,
        name="pool_branch",
        grid=(bsz, per_seq),
        in_specs=[
            pl.BlockSpec((ts, db), lambda bb, i: (bb * per_seq + i, col)),
            pl.BlockSpec((n_groups, pg, pg), lambda bb, i: (0, 0, 0)),
            vec, vec,
        ],
        out_specs=pl.BlockSpec((ts, db), lambda bb, i: (bb * per_seq + i, 0)),
        out_shape=jax.ShapeDtypeStruct((bsz * seq, db), BF16),
        scratch_shapes=[pltpu.VMEM((ts + POOL_HALO, db), F32)],
        compiler_params=_params(("arbitrary", "arbitrary")),
    )(proj, w.astype(BF16), b.reshape(1, db), scale.reshape(1, db))


def _conv_kernel(val_ref, gate_ref, w_ref, b_ref, g_ref, beta_ref, o_ref, ue_ref, sh_ref, *, row_block):
    ts, db = val_ref.shape
    taps = w_ref.shape[0]
    halo = CONV_HALO
    i = pl.program_id(1)

    @pl.when(i == 0)
    def _():
        ue_ref[0:halo, :] = jnp.zeros((halo, db), F32)

    @pl.when(i != 0)
    def _():
        ue_ref[0:halo, :] = ue_ref[ts:ts + halo, :]

    ue_ref[halo:halo + ts, :] = val_ref[...].astype(F32) * _sigmoid(gate_ref[...].astype(F32))
    span = sh_ref.shape[1]
    for p in range(1, SUBLANES):
        sh_ref[p - 1] = ue_ref[p:p + span, :]
    base = halo - (taps - 1)
    for rb in range(ts // row_block):
        r0 = rb * row_block
        acc = jnp.zeros((row_block, db), F32) + b_ref[...]
        for j in range(taps):
            a, p = divmod(base + j, SUBLANES)
            row = r0 + a * SUBLANES
            window = ue_ref[row:row + row_block, :] if p == 0 else sh_ref[p - 1, row:row + row_block, :]
            acc = acc + w_ref[j:j + 1, :] * window
        mu = jnp.mean(acc, axis=-1, keepdims=True)
        cen = acc - mu
        var = jnp.mean(cen * cen, axis=-1, keepdims=True)
        y = cen * lax.rsqrt(var + EPS) * g_ref[...] + beta_ref[...]
        o_ref[r0:r0 + row_block, :] = (y * _sigmoid(y)).astype(o_ref.dtype)


def _conv_branch(proj, bsz, seq, col_val, col_gate, w, b, g, beta):
    taps, db = w.shape
    assert taps - 1 <= CONV_HALO
    ts = _tile(seq, 256)
    per_seq = seq // ts
    row = lambda v: v.reshape(1, db)
    vec = pl.BlockSpec((1, db), lambda bb, i: (0, 0))
    return pl.pallas_call(
        functools.partial(_conv_kernel, row_block=_tile(ts, 32)),
        name="conv_branch",
        grid=(bsz, per_seq),
        in_specs=[
            pl.BlockSpec((ts, db), lambda bb, i: (bb * per_seq + i, col_val)),
            pl.BlockSpec((ts, db), lambda bb, i: (bb * per_seq + i, col_gate)),
            pl.BlockSpec((taps, db), lambda bb, i: (0, 0)),
            vec, vec, vec,
        ],
        out_specs=pl.BlockSpec((ts, db), lambda bb, i: (bb * per_seq + i, 0)),
        out_shape=jax.ShapeDtypeStruct((bsz * seq, db), BF16),
        scratch_shapes=[
            pltpu.VMEM((ts + CONV_HALO, db), F32),
            pltpu.VMEM((SUBLANES - 1, ts + CONV_HALO - SUBLANES, db), F32),
        ],
        compiler_params=_params(("arbitrary", "arbitrary")),
    )(proj, proj, w, row(b), row(g), row(beta))


def _qk_prep_kernel(q_ref, k_ref, cos_ref, sa_ref, sb_ref, seg_ref, qg_ref, kg_ref, qo_ref, kt_ref, kn_ref):
    ts, db = q_ref.shape
    cos, sa, sb = cos_ref[...], sa_ref[...], sb_ref[...]

    def norm_rope(x, g):
        ms = jnp.dot((x * x).astype(BF16), seg_ref[...], preferred_element_type=F32)
        xn = x * lax.rsqrt(ms + EPS) * g
        return (xn * cos + pltpu.roll(xn, HEAD_W - ROPE_HALF, 1) * sa + pltpu.roll(xn, ROPE_HALF, 1) * sb)

    for h in range(db // HEAD_W):
        sl = slice(h * HEAD_W, (h + 1) * HEAD_W)
        q = norm_rope(q_ref[:, sl].astype(F32), qg_ref[...])
        qo_ref[:, sl] = (q * (LOG2_E / math.sqrt(HEAD_DIM))).astype(qo_ref.dtype)
        kn_ref[:, sl] = norm_rope(k_ref[:, sl].astype(F32), kg_ref[...])
    kt_ref[...] = kn_ref[...].T.astype(kt_ref.dtype)


def _qk_prep(proj, bsz, seq, col_q, col_k, db, cos_t, sa_t, sb_t, seg, qg, kg):
    ts = _tile(seq, 256)
    per_seq = seq // ts
    tab = pl.BlockSpec((ts, HEAD_W), lambda b, i: (b * per_seq + i, 0))
    vec = pl.BlockSpec((1, HEAD_W), lambda b, i: (0, 0))
    return pl.pallas_call(
        _qk_prep_kernel,
        name="qk_prep",
        grid=(bsz, per_seq),
        in_specs=[
            pl.BlockSpec((ts, db), lambda b, i: (b * per_seq + i, col_q)),
            pl.BlockSpec((ts, db), lambda b, i: (b * per_seq + i, col_k)),
            tab, tab, tab,
            pl.BlockSpec((HEAD_W, HEAD_W), lambda b, i: (0, 0)),
            vec, vec,
        ],
        out_specs=[
            pl.BlockSpec((ts, db), lambda b, i: (b * per_seq + i, 0)),
            pl.BlockSpec((None, db, ts), lambda b, i: (b, 0, i)),
        ],
        out_shape=[
            jax.ShapeDtypeStruct((bsz * seq, db), BF16),
            jax.ShapeDtypeStruct((bsz, db, seq), BF16),
        ],
        scratch_shapes=[pltpu.VMEM((ts, db), F32)],
        compiler_params=_params(("arbitrary", "arbitrary")),
    )(proj, proj, cos_t, sa_t, sb_t, seg, qg, kg)


def _flash_kernel(q_ref, kt_ref, v_ref, lq1_ref, lk1_ref, lq2_ref, lk2_ref, sg_ref, o_ref,
                  m_ref, l_ref, acc_ref, p_ref, s_ref, *, lambda_init, tk):
    tq = q_ref.shape[0]
    rows = 2 * tq
    i = pl.program_id(2)
    lane = lax.broadcasted_iota(jnp.int32, (tq, HEAD_W), 1)
    q = q_ref[...]
    zero = jnp.zeros_like(q)
    q2 = jnp.concatenate([jnp.where(lane < HEAD_DIM, q, zero), jnp.where(lane >= HEAD_DIM, q, zero)], axis=0)
    m_ref[...] = jnp.full(m_ref.shape, NEG_BIG, F32)
    l_ref[...] = jnp.zeros(l_ref.shape, F32)
    acc_ref[...] = jnp.zeros(acc_ref.shape, F32)
    p_ref[...] = jnp.zeros(p_ref.shape, p_ref.dtype)

    def pv(j):
        k0 = pl.multiple_of(j * tk, tk)
        return jnp.dot(p_ref[...], v_ref[pl.ds(k0, tk), :], preferred_element_type=F32)

    def scores(j):
        k0 = pl.multiple_of(j * tk, tk)
        return jnp.dot(q2, kt_ref[:, pl.ds(k0, tk)], preferred_element_type=F32)

    n_full = (i * tq) // tk
    n_all = ((i + 1) * tq + tk - 1) // tk
    s_ref[...] = scores(0)

    def step(j, masked):
        k0 = pl.multiple_of(j * tk, tk)
        s = s_ref[...]
        pv_prev = pv(jnp.maximum(j - 1, 0))
        s_ref[...] = scores(jnp.minimum(j + 1, n_all - 1))
        if masked:
            qpos = i * tq + lax.broadcasted_iota(jnp.int32, (rows, tk), 0) % tq
            kpos = k0 + lax.broadcasted_iota(jnp.int32, (rows, tk), 1)
            s = jnp.where(kpos <= qpos, s, NEG_BIG)
        cols = [s[:, cc * LANES:(cc + 1) * LANES] for cc in range(tk // LANES)]
        m_prev = m_ref[...]
        m_next = jnp.maximum(m_prev, jnp.max(functools.reduce(jnp.maximum, cols), axis=-1, keepdims=True))
        alpha = jnp.exp2(m_prev - m_next)
        ps = [jnp.exp2(col - m_next) for col in cols]
        l_ref[...] = alpha * l_ref[...] + functools.reduce(jnp.add, ps)
        acc_ref[...] = alpha * (acc_ref[...] + pv_prev)
        p_ref[...] = jnp.concatenate([x.astype(p_ref.dtype) for x in ps], axis=1)
        m_ref[...] = m_next

    def full_body(j, c):
        step(j, False)
        return c

    def diag_body(j, c):
        step(j, True)
        return c

    lax.fori_loop(0, n_full, full_body, 0)
    lax.fori_loop(n_full, n_all, diag_body, 0)

    lam = (jnp.exp(jnp.sum(lq1_ref[...] * lk1_ref[...], axis=-1, keepdims=True))
           - jnp.exp(jnp.sum(lq2_ref[...] * lk2_ref[...], axis=-1, keepdims=True)) + lambda_init)
    o_all = (acc_ref[...] + pv(n_all - 1)) / jnp.sum(l_ref[...], axis=-1, keepdims=True)
    o = o_all[0:tq, :] - lam * o_all[tq:rows, :]
    o = o * lax.rsqrt(jnp.mean(o * o, axis=-1, keepdims=True) + EPS) * sg_ref[...]
    o_ref[...] = (o * (1.0 - lambda_init)).astype(o_ref.dtype)


def _flash(qn, kt, proj, bsz, seq, col_v, lq1, lk1, lq2, lk2, sub_g, lambda_init):
    db = qn.shape[1]
    heads = db // HEAD_W
    tq = _tile(seq, 512)
    tk = _tile(seq, 512)
    per_seq = seq // tq
    vec64 =pl.BlockSpec((1, HEAD_DIM), lambda b, h, i: (0, 0))
    row64 = lambda v: v.reshape(1, HEAD_DIM)
    return pl.pallas_call(
        functools.partial(_flash_kernel, lambda_init=lambda_init, tk=tk),
        name="diff_attn",
        grid=(bsz, heads, per_seq),
        in_specs=[
            pl.BlockSpec((tq, HEAD_W), lambda b, h, i: (b * per_seq + i, h)),
            pl.BlockSpec((None, HEAD_W, seq), lambda b, h, i: (b, h, 0)),
            pl.BlockSpec((seq, HEAD_W), lambda b, h, i: (b, col_v * heads + h)),
            vec64, vec64, vec64, vec64,
            pl.BlockSpec((1, HEAD_W), lambda b, h, i: (0, 0)),
        ],
        out_specs=pl.BlockSpec((tq, HEAD_W), lambda b, h, i: (b * per_seq + i, h)),
        out_shape=jax.ShapeDtypeStruct((bsz * seq, db), BF16),
        scratch_shapes=[pltpu.VMEM((2 * tq, HEAD_W), F32)] * 3 + [
            pltpu.VMEM((2 * tq, tk), BF16),
            pltpu.VMEM((2 * tq, tk), F32),
        ],
        compiler_params=_params(("arbitrary", "arbitrary", "arbitrary")),
    )(qn, kt, proj, row64(lq1), row64(lk1), row64(lq2), row64(lk2), sub_g.reshape(1, HEAD_W))


def _rope_tables(positions):
    inv_freq = ROPE_THETA ** (-jnp.arange(0, ROPE_DIM, 2, dtype=F32) / ROPE_DIM)
    lane = jnp.arange(HEAD_W, dtype=jnp.int32) % HEAD_DIM
    freq = jnp.where(lane < ROPE_DIM, jnp.take(inv_freq, lane % ROPE_HALF), 0.0)
    ang = positions.astype(F32).reshape(-1, 1) * freq[None, :]
    sin = jnp.sin(ang)
    return jnp.cos(ang), jnp.where(lane < ROPE_HALF, -sin, 0.0), jnp.where(lane >= ROPE_HALF, sin, 0.0)


def _merge_kernel(h_ref, ya_ref, yb_ref, yc_ref, yd_ref, wa_ref, wb_ref, wc_ref, wd_ref,
                  bg_ref, wbr_ref, o_ref):
    h = h_ref[...]
    acc = None
    for bi, (y_ref, wg_ref) in enumerate(((ya_ref, wa_ref), (yb_ref, wb_ref), (yc_ref, wc_ref), (yd_ref, wd_ref))):
        gate = _sigmoid(jnp.dot(h, wg_ref[...], preferred_element_type=F32) + bg_ref[bi:bi + 1, :])
        term = gate * jnp.dot(y_ref[...], wbr_ref[bi], preferred_element_type=F32)
        acc = term if acc is None else acc + term
    o_ref[...] = acc.astype(o_ref.dtype)


def _merge(h, ys, w_in16, layer, n_mix, b_gate, w_branch16):
    n, d = h.shape
    db = ys[0].shape[1]
    n_branch = len(ys)
    tm = _tile(n, 1024)
    tn = _tile(d, 256)
    gate_blk = [(n_mix + b * d) // tn for b in range(n_branch)]
    once = pl.Buffered(1)
    wspecs = [pl.BlockSpec((None, d, tn), functools.partial(lambda j, i, base: (layer, 0, base + j), base=gb),
                           pipeline_mode=once) for gb in gate_blk]
    yspec = pl.BlockSpec((tm, db), lambda j, i: (i, 0))
    return pl.pallas_call(
        _merge_kernel,
        name="gated_merge",
        grid=(d // tn, n // tm),
        in_specs=[pl.BlockSpec((tm, d), lambda j, i: (i, 0))] + [yspec] * n_branch + wspecs + [
            pl.BlockSpec((n_branch, tn), lambda j, i: (0, j)),
            pl.BlockSpec((None, n_branch, db, tn), lambda j, i: (layer, 0, 0, j), pipeline_mode=once),
        ],
        out_specs=pl.BlockSpec((tm, tn), lambda j, i: (i, j)),
        out_shape=jax.ShapeDtypeStruct((n, d), BF16),
        compiler_params=_params(("arbitrary", "arbitrary")),
    )(h, *ys, *([w_in16] * n_branch), b_gate, w_branch16)


def _resid_matmul_kernel(a_ref, w_ref, x_ref, mod_ref, o_ref, *, gate_row):
    y = jnp.dot(a_ref[...], w_ref[...], preferred_element_type=F32)
    o_ref[...] = x_ref[...] + mod_ref[gate_row:gate_row + 1, :] * y


def _resid_matmul(a, w16, layer, x2, mod_l, seq, gate_row):
    n, k = a.shape
    d = w16.shape[2]
    tm = _tile(seq, 1024)
    tn = _tile(d, 1024)
    per_seq = seq // tm
    return pl.pallas_call(
        functools.partial(_resid_matmul_kernel, gate_row=gate_row),
        name="out_proj_resid",
        grid=(n // tm, d // tn),
        in_specs=[
            pl.BlockSpec((tm, k), lambda i, j: (i, 0)),
            pl.BlockSpec((None, k, tn), lambda i, j: (layer, 0, j)),
            pl.BlockSpec((tm, tn), lambda i, j: (i, j)),
            pl.BlockSpec((None, 6, tn), lambda i, j: (i // per_seq, 0, j)),
        ],
        out_specs=pl.BlockSpec((tm, tn), lambda i, j: (i, j)),
        out_shape=jax.ShapeDtypeStruct((n, d), F32),
        compiler_params=_params(("arbitrary", "arbitrary")),
    )(a, w16, x2, mod_l)


def _pack_halves(v):
    w = v.shape[1] // 2
    bits = lambda t: pltpu.bitcast(t.astype(BF16).astype(F32), jnp.uint32)
    return bits(v[:, w:]) | (bits(v[:, :w]) >> 16)


def _unpack_halves(p):
    return (pltpu.bitcast(p << 16, F32), pltpu.bitcast(p & jnp.uint32(0xFFFF0000), F32))


def _store_tiled_rows(ref3, val2):
    for s in range(ref3.shape[0]):
        ref3[s] = val2[:, s * LANES:(s + 1) * LANES]


def _load_tiled_rows(ref3):
    return jnp.concatenate([ref3[s] for s in range(ref3.shape[0])], axis=1)


def _router_kernel(x_ref, g_ref, mod_ref, wr_ref, br_ref, h_ref, r_ref, *, n_groups, per_group):
    h2 = _modnorm(x_ref[...], g_ref[...], mod_ref[...], 3, 4)
    _store_tiled_rows(h_ref, _pack_halves(h2))
    logits = jnp.dot(h2.astype(BF16), wr_ref[...], preferred_element_type=F32) + br_ref[...]
    lane = lax.broadcasted_iota(jnp.int32, logits.shape, 1)
    n_exp = n_groups * per_group
    is_g = lane < n_groups
    gl = jnp.where(is_g, logits, NEG_BIG)
    gmax = jnp.max(gl, axis=-1, keepdims=True)
    g_idx = jnp.min(jnp.where(gl == gmax, lane, LANES), axis=-1, keepdims=True)
    g_top = 1.0 / jnp.sum(jnp.where(is_g, jnp.exp(gl - gmax), 0.0), axis=-1, keepdims=True)
    lo = n_groups + g_idx * per_group
    in_grp = jnp.logical_and(lane >= lo, lane < lo + per_group)
    el = jnp.where(in_grp, logits, NEG_BIG)
    v1 = jnp.max(el, axis=-1, keepdims=True)
    i1 = jnp.min(jnp.where(el == v1, lane, LANES), axis=-1, keepdims=True)
    el2 = jnp.where(lane == i1, NEG_BIG, el)
    v2 = jnp.max(el2, axis=-1, keepdims=True)
    i2 = jnp.min(jnp.where(el2 == v2, lane, LANES), axis=-1, keepdims=True)
    e = jnp.exp(v2 - v1)
    w1 = g_top / (1.0 + e)
    w2 = g_top * e / (1.0 + e)
    del n_exp
    out = jnp.where(lane == 0, (i1 - n_groups).astype(F32),
                    jnp.where(lane == 1, (i2 - n_groups).astype(F32),
                              jnp.where(lane == 2, w1, jnp.where(lane == 3, w2, 0.0))))
    r_ref[...] = out


def _router(x2, g, mod_l, seq, w_router16, b_router, n_groups, per_group):
    n, d = x2.shape
    tm = _tile(seq, 256)
    per_seq = seq // tm
    return pl.pallas_call(
        functools.partial(_router_kernel, n_groups=n_groups, per_group=per_group),
        name="moe_router",
        grid=(n // tm,),
        in_specs=[
            pl.BlockSpec((tm, d), lambda i: (i, 0)),
            pl.BlockSpec((1, d), lambda i: (0, 0)),
            pl.BlockSpec((None, 6, d), lambda i: (i // per_seq, 0, 0)),
            pl.BlockSpec((d, LANES), lambda i: (0, 0)),
            pl.BlockSpec((1, LANES), lambda i: (0, 0)),
        ],
        out_specs=[
            pl.BlockSpec((d // (2 * LANES), tm, LANES), lambda i: (0, i, 0)),
            pl.BlockSpec((tm, LANES), lambda i: (i, 0)),
        ],
        out_shape=[
            jax.ShapeDtypeStruct((d // (2 * LANES), n, LANES), jnp.uint32),
            jax.ShapeDtypeStruct((n, LANES), F32),
        ],
        compiler_params=_params(("arbitrary",)),
    )(x2, g.reshape(1, d), mod_l, w_router16, b_router)


GATHER_UNROLL = 8
FFN_GATHER_SLOTS = 3


def _start_row_gather(idx_ref, base, n_rows, src_hbm, dst, sem):
    def body(g, c):
        r0 = pl.multiple_of(g * GATHER_UNROLL, GATHER_UNROLL)
        for k in range(GATHER_UNROLL):
            row = idx_ref[base + r0 + k]
            pltpu.make_async_copy(src_hbm.at[:, pl.ds(row, 1), :], dst.at[:, pl.ds(r0 + k, 1), :],
                                  sem).start(priority=k % 2)
        return c
    lax.fori_loop(0, n_rows // GATHER_UNROLL, body, 0)


def _wait_row_gather(src_hbm, dst, sem):
    pltpu.make_async_copy(src_hbm.at[:, pl.ds(0, dst.shape[1]), :], dst, sem).wait()


def _ffn_kernel(tile_exp_ref, n_tiles_ref, ring_ref, src_ref, h_hbm, w1_hbm, w3_hbm, w2_hbm, y_ref,
                xbuf, sem, w1buf, w3buf, w2buf, wsem, *, tm, layer):
    t = pl.program_id(0)
    n_tiles = n_tiles_ref[0]
    max_tiles = pl.num_programs(0)
    n_slots = xbuf.shape[0]
    ahead = n_slots - 1
    slot = t % n_slots

    def start_tile(tile):
        s = tile % n_slots
        _start_row_gather(src_ref, tile * tm, tm, h_hbm, xbuf.at[s], sem.at[s])

    for first in range(ahead):
        @pl.when(jnp.logical_and(t == 0, first < n_tiles))
        def _():
            start_tile(first)

    @pl.when(t + ahead < n_tiles)
    def _():
        start_tile(t + ahead)

    def weight_copies(expert, s):
        return [pltpu.make_async_copy(src.at[layer, expert], dst.at[s], wsem.at[s])
                for src, dst in ((w1_hbm, w1buf), (w3_hbm, w3buf), (w2_hbm, w2buf))]

    @pl.when(t < n_tiles)
    def _():
        ws = ring_ref[max_tiles + t]

        @pl.when(ring_ref[t] == 1)
        def _():
            @pl.when(t == 0)
            def _():
                for c in weight_copies(tile_exp_ref[0], ws):
                    c.start()

            next_expert = ring_ref[2 * max_tiles + t]

            @pl.when(next_expert >= 0)
            def _():
                for c in weight_copies(next_expert, 1 - ws):
                    c.start()

            for c in weight_copies(tile_exp_ref[t], ws):
                c.wait()

        _wait_row_gather(h_hbm, xbuf.at[slot], sem.at[slot])
        x_lo, x_hi = (v.astype(BF16) for v in _unpack_halves(_load_tiled_rows(xbuf.at[slot])))
        half = x_lo.shape[1]

        def up_proj(wbuf):
            return (jnp.dot(x_lo, wbuf[ws, 0:half, :], preferred_element_type=F32)
                    + jnp.dot(x_hi, wbuf[ws, half:2 * half, :], preferred_element_type=F32))

        a = up_proj(w1buf)
        b = up_proj(w3buf)
        hid = (a * _sigmoid(a) * b).astype(BF16)
        _store_tiled_rows(y_ref, _pack_halves(jnp.dot(hid, w2buf[ws], preferred_element_type=F32)))

    @pl.when(t >= n_tiles)
    def _():
        y_ref[...] = jnp.zeros_like(y_ref)


def _ffn(h2, tile_expert, n_tiles, ring, src_token, w1, w3, w2, layer, tm):
    d, de = w1.shape[2], w1.shape[3]
    n_chunks = h2.shape[0]
    max_tiles = tile_expert.shape[0]
    any_space = pl.BlockSpec(memory_space=pl.ANY)
    grid_spec = pltpu.PrefetchScalarGridSpec(
        num_scalar_prefetch=4,
        grid=(max_tiles,),
        in_specs=[any_space] * 4,
        out_specs=pl.BlockSpec((n_chunks, tm, LANES), lambda t, te, nt, rg, src: (0, t, 0)),
        scratch_shapes=[
            pltpu.VMEM((FFN_GATHER_SLOTS, n_chunks, tm, LANES), jnp.uint32),
            pltpu.SemaphoreType.DMA((FFN_GATHER_SLOTS,)),
            pltpu.VMEM((2, d, de), BF16),
            pltpu.VMEM((2, d, de), BF16),
            pltpu.VMEM((2, de, d), BF16),
            pltpu.SemaphoreType.DMA((2,)),
        ],
    )
    return pl.pallas_call(
        functools.partial(_ffn_kernel, tm=tm, layer=layer),
        name="moe_ffn",
        grid_spec=grid_spec,
        out_shape=jax.ShapeDtypeStruct((n_chunks, max_tiles * tm, LANES), jnp.uint32),
        compiler_params=_params(("arbitrary",), disable_bounds_checks=True),
    )(tile_expert, n_tiles, ring, src_token, h2, w1, w3, w2)


def _combine_kernel(pos_ref, y_hbm, x_ref, mod_ref, w_ref, o_ref, ybuf, sem, *, tc):
    t = pl.program_id(0)
    n_steps = pl.num_programs(0)
    slot = t % 2

    def start_step(step, s):
        _start_row_gather(pos_ref, step * (TOP_K * tc), TOP_K * tc, y_hbm, ybuf.at[s], sem.at[s])

    @pl.when(t == 0)
    def _():
        start_step(0, 0)

    @pl.when(t + 1 < n_steps)
    def _():
        start_step(t + 1, 1 - slot)

    _wait_row_gather(y_hbm, ybuf.at[slot], sem.at[slot])
    half = ybuf.shape[1] * ybuf.shape[3]
    chunk = _tile(tc, 2 * SUBLANES)
    for r0 in range(0, tc, chunk):
        w = w_ref[r0:r0 + chunk, :]
        first = _unpack_halves(_load_tiled_rows(ybuf.at[slot, :, r0:r0 + chunk, :]))
        second = _unpack_halves(_load_tiled_rows(ybuf.at[slot, :, tc + r0:tc + r0 + chunk, :]))
        for part in range(2):
            cols = slice(part * half, (part + 1) * half)
            moe = w[:, 2:3] * first[part] + w[:, 3:4] * second[part]
            o_ref[r0:r0 + chunk, cols] = x_ref[r0:r0 + chunk, cols] + mod_ref[5:6, cols] * moe


def _combine(pos_steps, y_sorted, x2, mod_l, route, seq, tc):
    n, d = x2.shape
    per_seq = seq // tc
    grid_spec = pltpu.PrefetchScalarGridSpec(
        num_scalar_prefetch=1,
        grid=(n // tc,),
        in_specs=[
            pl.BlockSpec(memory_space=pl.ANY),
            pl.BlockSpec((tc, d), lambda t, pos: (t, 0)),
            pl.BlockSpec((None, 6, d), lambda t, pos: (t // per_seq, 0, 0)),
            pl.BlockSpec((tc, LANES), lambda t, pos: (t, 0)),
        ],
        out_specs=pl.BlockSpec((tc, d), lambda t, pos: (t, 0)),
        scratch_shapes=[
            pltpu.VMEM((2, y_sorted.shape[0], TOP_K * tc, LANES), jnp.uint32),
            pltpu.SemaphoreType.DMA((2,)),
        ],
    )
    return pl.pallas_call(
        functools.partial(_combine_kernel, tc=tc),
        name="moe_combine",
        grid_spec=grid_spec,
        out_shape=jax.ShapeDtypeStruct((n, d), F32),
        compiler_params=_params(("arbitrary",), disable_bounds_checks=True),
    )(pos_steps, y_sorted, x2, mod_l, route)


def _moe_plan(route, n_exp, tm, tc):
    n = route.shape[0]
    eid = route[:, 0:TOP_K].astype(jnp.int32)
    flat_e = eid.reshape(-1)
    onehot = (flat_e[:, None] == jnp.arange(n_exp, dtype=jnp.int32)[None, :]).astype(jnp.int32)
    csum = jnp.cumsum(onehot, axis=0)
    rank = jnp.sum(csum * onehot, axis=1) - 1
    counts = csum[-1]
    tiles = (counts + tm - 1) // tm
    tile_end = jnp.cumsum(tiles)
    tile_start = tile_end - tiles
    n_tiles = tile_end[-1]
    pos = jnp.sum(onehot * (tile_start * tm)[None, :], axis=1) + rank
    max_tiles = (TOP_K * n) // tm + n_exp
    token = jnp.arange(TOP_K * n, dtype=jnp.int32) // TOP_K
    src_token = jnp.zeros((max_tiles * tm,), jnp.int32).at[pos].set(token)
    tix = jnp.minimum(jnp.arange(max_tiles, dtype=jnp.int32), n_tiles - 1)
    tile_expert = jnp.sum((tile_end[None, :] <= tix[:, None]).astype(jnp.int32), axis=1)
    t_idx = jnp.arange(max_tiles, dtype=jnp.int32)
    prev_expert = jnp.concatenate([jnp.full((1,), -1, jnp.int32), tile_expert[:-1]])
    first = jnp.logical_and(t_idx < n_tiles, tile_expert != prev_expert)
    w_slot = (jnp.cumsum(first.astype(jnp.int32)) - 1) % 2
    first_at_or_after = lax.cummin(jnp.where(first, t_idx, max_tiles), axis=0, reverse=True)
    next_first = jnp.concatenate([first_at_or_after[1:], jnp.full((1,), max_tiles, jnp.int32)])
    next_expert = jnp.where(next_first < max_tiles, tile_expert[jnp.minimum(next_first, max_tiles - 1)], -1)
    ring = jnp.stack([first.astype(jnp.int32), w_slot.astype(jnp.int32), next_expert.astype(jnp.int32)])
    pos_steps = pos.reshape(n // tc, tc, TOP_K).transpose(0, 2, 1).reshape(-1)
    return (tile_expert, n_tiles.reshape(1).astype(jnp.int32), ring.reshape(-1), src_token,
            pos_steps.astype(jnp.int32))


def kernel(x, c, positions, ada_w, ada_b, norm1_g, norm2_g, w_in, b_gate, lru_conv_w, lru_conv_b, lru_wr, lru_br, lru_wi, lru_bi, lru_lambda, pool_w, pool_b, pool_scale, q_norm_g, k_norm_g, lam_q1, lam_k1, lam_q2, lam_k2, subln_g, cv_dw_w, cv_dw_b, cv_ln_g, cv_ln_b, w_branch, w_out, router_g_w, router_g_b, router_e_w, router_e_b, moe_w1, moe_w3, moe_w2):
    bsz, seq, d = x.shape
    depth = ada_w.shape[0]
    db = lru_conv_w.shape[2]
    n_branch = w_branch.shape[1]
    n_mix = w_in.shape[2] - n_branch * d
    assert n_mix == 8 * db and q_norm_g.shape[1] == HEAD_DIM
    n_groups = router_g_w.shape[2]
    n_exp = router_e_w.shape[2]
    per_group = n_exp // n_groups
    assert n_groups + n_exp <= LANES
    n = bsz * seq
    col = {name: idx for idx, name in enumerate(("lru_x", "lru_gate", "pool", "q", "k", "v", "glu_val", "glu_gate"))}

    mod = _ada_mod(c, ada_w, ada_b).reshape(depth, bsz, 6, d)
    cos_t, sa_t, sb_t = _rope_tables(positions)
    seg = jnp.kron(jnp.eye(HEAD_W // HEAD_DIM, dtype=F32), jnp.full((HEAD_DIM, HEAD_DIM), 1.0 / HEAD_DIM, F32)).astype(BF16)
    dup = lambda g: jnp.concatenate([g, g]).reshape(1, HEAD_W)
    ffn_tm = _tile(TOP_K * n, 256)
    comb_tc = _tile(seq, 512)

    w_in16, w_branch16, w_out16 = w_in.astype(BF16), w_branch.astype(BF16), w_out.astype(BF16)
    moe_w1_16, moe_w3_16, moe_w2_16 = moe_w1.astype(BF16), moe_w3.astype(BF16), moe_w2.astype(BF16)

    x2 = x.reshape(n, d)
    for l in range(depth):
        mod_l = mod[l]
        h = _norm(x2, norm1_g[l], mod_l, seq, 0, 1)
        proj = _matmul(h, w_in16, l, n_mix)
        y_a = _lru_branch(proj, bsz, seq, col["lru_x"], col["lru_gate"], lru_conv_w[l], lru_conv_b[l],
                          lru_wr[l], lru_br[l], lru_wi[l], lru_bi[l], lru_lambda[l])
        y_b = _pool_branch(proj, bsz, seq, col["pool"], pool_w[l], pool_b[l], pool_scale[l])
        qn, kt = _qk_prep(proj, bsz, seq, col["q"], col["k"], db, cos_t, sa_t, sb_t, seg,
                          dup(q_norm_g[l]), dup(k_norm_g[l]))
        lambda_init = 0.8 - 0.6 * math.exp(-0.3 * l)
        y_c = _flash(qn, kt, proj, bsz, seq, col["v"], lam_q1[l], lam_k1[l], lam_q2[l], lam_k2[l],
                     subln_g[l], lambda_init)
        y_d = _conv_branch(proj, bsz, seq, col["glu_val"], col["glu_gate"], cv_dw_w[l], cv_dw_b[l],
                           cv_ln_g[l], cv_ln_b[l])
        merged = _merge(h, (y_a, y_b, y_c, y_d), w_in16, l, n_mix, b_gate[l], w_branch16)
        x2 = _resid_matmul(merged, w_out16, l, x2, mod_l, seq, 2)

        w_router = jnp.concatenate([router_g_w[l], router_e_w[l]], axis=1)
        w_router = jnp.pad(w_router, ((0, 0), (0, LANES - w_router.shape[1]))).astype(BF16)
        b_router = jnp.pad(jnp.concatenate([router_g_b[l], router_e_b[l]]), (0, LANES - n_groups - n_exp)).reshape(1, LANES)
        h2, route = _router(x2, norm2_g[l], mod_l, seq, w_router, b_router, n_groups, per_group)
        tile_expert, n_tiles, ring, src_token, pos_steps = _moe_plan(route, n_exp, ffn_tm, comb_tc)
        y_sorted = _ffn(h2, tile_expert, n_tiles, ring, src_token, moe_w1_16, moe_w3_16, moe_w2_16, l, ffn_tm)
        x2 = _combine(pos_steps, y_sorted, x2, mod_l, route, seq, comb_tc)
    return x2.reshape(bsz, seq, d)
```

```python
import functools
import math

import jax
import jax.numpy as jnp
from jax import lax
from jax.experimental import pallas as pl
from jax.experimental.pallas import tpu as pltpu

EPS = 1e-6
LRU_C = 8.0
LRU_CONV_TAPS = 4
POOL_WINDOWS = (2, 4, 8, 16)
POOL_HALO = 16
CONV_HALO = 32
HEAD_DIM = 64
HEAD_W = 2 * HEAD_DIM
ROPE_DIM = HEAD_DIM // 4
ROPE_HALF = ROPE_DIM // 2
ROPE_THETA = 500000.0
TOP_K = 2
NEG_BIG = -1e30
LOG2_E = math.log2(math.e)
SUBLANES = 8
LANES = 128
VMEM_LIMIT_BYTES = 56 * 1024 * 1024

BF16 = jnp.bfloat16
F32 = jnp.float32


def _tile(n, pref):
    t = min(n, pref)
    assert n % t == 0, (n, pref)
    return t


def _params(sem, **kw):
    return pltpu.CompilerParams(dimension_semantics=sem, vmem_limit_bytes=VMEM_LIMIT_BYTES, **kw)


def _sigmoid(x):
    return 1.0 / (1.0 + jnp.exp(-x))


def _ada_kernel(c_ref, w_ref, b_ref, o_ref):
    c = c_ref[...]
    c_act = (c * _sigmoid(c)).astype(BF16)
    o_ref[...] = jnp.dot(c_act, w_ref[...].astype(BF16), preferred_element_type=F32) + b_ref[...]


def _ada_mod(c, ada_w, ada_b):
    depth, d, n_out = ada_w.shape
    bsz = c.shape[0]
    tn = _tile(n_out, 1024)
    return pl.pallas_call(
        _ada_kernel,
        name="ada_mod",
        grid=(depth, n_out // tn),
        in_specs=[
            pl.BlockSpec((bsz, d), lambda l, j: (0, 0)),
            pl.BlockSpec((None, d, tn), lambda l, j: (l, 0, j)),
            pl.BlockSpec((None, 1, tn), lambda l, j: (l, 0, j)),
        ],
        out_specs=pl.BlockSpec((None, bsz, tn), lambda l, j: (l, 0, j)),
        out_shape=jax.ShapeDtypeStruct((depth, bsz, n_out), F32),
        compiler_params=_params(("arbitrary", "arbitrary")),
    )(c, ada_w, ada_b.reshape(depth, 1, n_out))


def _modnorm(x, g, mod, shift_row, scale_row):
    y = x * lax.rsqrt(jnp.mean(x * x, axis=-1, keepdims=True) + EPS) * g
    return y * (1.0 + mod[scale_row:scale_row + 1, :]) + mod[shift_row:shift_row + 1, :]


def _norm_kernel(x_ref, g_ref, mod_ref, o_ref, *, shift_row, scale_row):
    o_ref[...] = _modnorm(x_ref[...], g_ref[...], mod_ref[...], shift_row, scale_row).astype(o_ref.dtype)


def _norm(x2, g, mod_l, seq, shift_row, scale_row):
    n, d = x2.shape
    tm = _tile(seq, 512)
    per_seq = seq // tm
    return pl.pallas_call(
        functools.partial(_norm_kernel, shift_row=shift_row, scale_row=scale_row),
        name="mod_norm",
        grid=(n // tm,),
        in_specs=[
            pl.BlockSpec((tm, d), lambda i: (i, 0)),
            pl.BlockSpec((1, d), lambda i: (0, 0)),
            pl.BlockSpec((None, 6, d), lambda i: (i // per_seq, 0, 0)),
        ],
        out_specs=pl.BlockSpec((tm, d), lambda i: (i, 0)),
        out_shape=jax.ShapeDtypeStruct((n, d), BF16),
        compiler_params=_params(("arbitrary",)),
    )(x2, g.reshape(1, d), mod_l)


def _matmul_kernel(a_ref, w_ref, o_ref):
    o_ref[...] = jnp.dot(a_ref[...], w_ref[...], preferred_element_type=F32).astype(o_ref.dtype)


def _matmul(a, w, layer, n_cols):
    m, k = a.shape
    tm = _tile(m, 1024)
    tn = _tile(n_cols, 1024)
    return pl.pallas_call(
        _matmul_kernel,
        name="in_proj",
        grid=(m // tm, n_cols // tn),
        in_specs=[
            pl.BlockSpec((tm, k), lambda i, j: (i, 0)),
            pl.BlockSpec((None, k, tn), lambda i, j: (layer, 0, j)),
        ],
        out_specs=pl.BlockSpec((tm, tn), lambda i, j: (i, j)),
        out_shape=jax.ShapeDtypeStruct((m, n_cols), BF16),
        compiler_params=_params(("arbitrary", "arbitrary")),
    )(a, w)


def _gelu_tanh(x):
    return 0.5 * x * (1.0 + jnp.tanh(math.sqrt(2.0 / math.pi) * (x + 0.044715 * (x * x * x))))


def _lru_kernel(x_ref, gate_ref, cw_ref, cb_ref, wr_ref, br_ref, wi_ref, bi_ref, lam_ref, o_ref,
                xe_ref, a_ref, u_ref, carry_ref):
    ts, db = x_ref.shape
    n_blocks, blk = wr_ref.shape[0], wr_ref.shape[1]
    halo = SUBLANES
    first = pl.program_id(1) == 0

    @pl.when(first)
    def _():
        xe_ref[0:halo, :] = jnp.zeros((halo, db), F32)
        carry_ref[...] = jnp.zeros_like(carry_ref)

    @pl.when(jnp.logical_not(first))
    def _():
        xe_ref[0:halo, :] = xe_ref[ts:ts + halo, :]

    xe_ref[halo:halo + ts, :] = x_ref[...].astype(F32)

    xa = jnp.zeros((ts, db), F32) + cb_ref[...]
    for j in range(LRU_CONV_TAPS):
        off = halo - (LRU_CONV_TAPS - 1) + j
        xa = xa + cw_ref[j:j + 1, :] * xe_ref[off:off + ts, :]

    xa16 = xa.astype(BF16)
    for hb in range(n_blocks):
        sl = slice(hb * blk, (hb + 1) * blk)
        a_ref[:, sl] = jnp.dot(xa16[:, sl], wr_ref[hb], preferred_element_type=F32)
        u_ref[:, sl] = jnp.dot(xa16[:, sl], wi_ref[hb], preferred_element_type=F32)
    r = _sigmoid(a_ref[...] + br_ref[...])
    ig = _sigmoid(u_ref[...] + bi_ref[...])
    z = -lam_ref[...]
    softplus = jnp.maximum(z, 0.0) + jnp.log(1.0 + jnp.exp(-jnp.abs(z)))
    a = jnp.exp((-LRU_C) * r * softplus)
    u = jnp.sqrt(1.0 - a * a) * (ig * xa)

    a = a.reshape(ts // SUBLANES, SUBLANES, db)
    u = u.reshape(ts // SUBLANES, SUBLANES, db)
    row = lax.broadcasted_iota(jnp.int32, a.shape, 1)
    for d in (1, 2, 4):
        keep = row >= d
        a_sh = jnp.where(keep, pltpu.roll(a, d, 1), 1.0)
        u_sh = jnp.where(keep, pltpu.roll(u, d, 1), 0.0)
        u = a * u_sh + u
        a = a * a_sh
    a_ref[...] = a.reshape(ts, db)
    u_ref[...] = u.reshape(ts, db)

    def body(c, h_prev):
        r0 = pl.multiple_of(c * SUBLANES, SUBLANES)
        h8 = u_ref[pl.ds(r0, SUBLANES), :] + a_ref[pl.ds(r0, SUBLANES), :] * h_prev
        u_ref[pl.ds(r0, SUBLANES), :] = h8
        return h8[SUBLANES - 1:SUBLANES, :]

    carry_ref[...] = lax.fori_loop(0, ts // SUBLANES, body, carry_ref[...])
    o_ref[...] = (_gelu_tanh(gate_ref[...].astype(F32)) * u_ref[...]).astype(o_ref.dtype)


def _lru_branch(proj, bsz, seq, col_x, col_gate, cw, cb, wr, br, wi, bi, lam):
    db = cw.shape[1]
    ts = _tile(seq, 256)
    per_seq = seq // ts
    n_blocks, blk = wr.shape[0], wr.shape[1]
    row = lambda v: v.reshape(1, db)
    vec = pl.BlockSpec((1, db), lambda b, i: (0, 0))
    wspec = pl.BlockSpec((n_blocks, blk, blk), lambda b, i: (0, 0, 0))
    return pl.pallas_call(
        _lru_kernel,
        name="lru_branch",
        grid=(bsz, per_seq),
        in_specs=[
            pl.BlockSpec((ts, db), lambda b, i: (b * per_seq + i, col_x)),
            pl.BlockSpec((ts, db), lambda b, i: (b * per_seq + i, col_gate)),
            pl.BlockSpec((LRU_CONV_TAPS, db), lambda b, i: (0, 0)),
            vec, wspec, vec, wspec, vec, vec,
        ],
        out_specs=pl.BlockSpec((ts, db), lambda b, i: (b * per_seq + i, 0)),
        out_shape=jax.ShapeDtypeStruct((bsz * seq, db), BF16),
        scratch_shapes=[
            pltpu.VMEM((ts + SUBLANES, db), F32),
            pltpu.VMEM((ts, db), F32),
            pltpu.VMEM((ts, db), F32),
            pltpu.VMEM((1, db), F32),
        ],
        compiler_params=_params(("arbitrary", "arbitrary")),
    )(proj, proj, cw, row(cb), wr.astype(BF16), row(br), wi.astype(BF16), row(bi), row(lam))


def _pool_kernel(x_ref, w_ref, b_ref, s_ref, o_ref, xe_ref):
    ts, db = x_ref.shape
    n_groups, pg = w_ref.shape[0], w_ref.shape[1]
    halo = POOL_HALO
    i = pl.program_id(1)

    @pl.when(i == 0)
    def _():
        xe_ref[0:halo, :] = jnp.zeros((halo, db), F32)

    @pl.when(i != 0)
    def _():
        xe_ref[0:halo, :] = xe_ref[ts:ts + halo, :]

    xe_ref[halo:halo + ts, :] = x_ref[...].astype(F32)
    t1 = (i * ts + 1 + lax.broadcasted_iota(jnp.int32, (ts, pg), 0)).astype(F32)
    for gi in range(n_groups):
        win = POOL_WINDOWS[gi]
        sl = slice(gi * pg, (gi + 1) * pg)
        x = xe_ref[halo:halo + ts, sl]
        acc = x
        for j in range(1, win):
            acc = acc + xe_ref[halo - j:halo - j + ts, sl]
        diff = acc / jnp.minimum(t1, float(win)) - x
        y = jnp.dot(diff.astype(BF16), w_ref[gi], preferred_element_type=F32) + b_ref[:, sl]
        o_ref[:, sl] = (y * s_ref[:, sl]).astype(o_ref.dtype)


def _pool_branch(proj, bsz, seq, col, w, b, scale):
    n_groups, pg = w.shape[0], w.shape[1]
    assert n_groups == len(POOL_WINDOWS)
    db = n_groups * pg
    ts = _tile(seq, 512)
    per_seq = seq // ts
    vec = pl.BlockSpec((1, db), lambda bb, i: (0, 0))
    return pl.pallas_call(
        _pool_kernel,
        name="pool_branch",
        grid=(bsz, per_seq),
        in_specs=[
            pl.BlockSpec((ts, db), lambda bb, i: (bb * per_seq + i, col)),
            pl.BlockSpec((n_groups, pg, pg), lambda bb, i: (0, 0, 0)),
            vec, vec,
        ],
        out_specs=pl.BlockSpec((ts, db), lambda bb, i: (bb * per_seq + i, 0)),
        out_shape=jax.ShapeDtypeStruct((bsz * seq, db), BF16),
        scratch_shapes=[pltpu.VMEM((ts + POOL_HALO, db), F32)],
        compiler_params=_params(("arbitrary", "arbitrary")),
    )(proj, w.astype(BF16), b.reshape(1, db), scale.reshape(1, db))


def _conv_kernel(val_ref, gate_ref, w_ref, b_ref, g_ref, beta_ref, o_ref, ue_ref, sh_ref, *, row_block):
    ts, db = val_ref.shape
    taps = w_ref.shape[0]
    halo = CONV_HALO
    i = pl.program_id(1)

    @pl.when(i == 0)
    def _():
        ue_ref[0:halo, :] = jnp.zeros((halo, db), F32)

    @pl.when(i != 0)
    def _():
        ue_ref[0:halo, :] = ue_ref[ts:ts + halo, :]

    ue_ref[halo:halo + ts, :] = val_ref[...].astype(F32) * _sigmoid(gate_ref[...].astype(F32))
    span = sh_ref.shape[1]
    for p in range(1, SUBLANES):
        sh_ref[p - 1] = ue_ref[p:p + span, :]
    base = halo - (taps - 1)
    for rb in range(ts // row_block):
        r0 = rb * row_block
        acc = jnp.zeros((row_block, db), F32) + b_ref[...]
        for j in range(taps):
            a, p = divmod(base + j, SUBLANES)
            row = r0 + a * SUBLANES
            window = ue_ref[row:row + row_block, :] if p == 0 else sh_ref[p - 1, row:row + row_block, :]
            acc = acc + w_ref[j:j + 1, :] * window
        mu = jnp.mean(acc, axis=-1, keepdims=True)
        cen = acc - mu
        var = jnp.mean(cen * cen, axis=-1, keepdims=True)
        y = cen * lax.rsqrt(var + EPS) * g_ref[...] + beta_ref[...]
        o_ref[r0:r0 + row_block, :] = (y * _sigmoid(y)).astype(o_ref.dtype)


def _conv_branch(proj, bsz, seq, col_val, col_gate, w, b, g, beta):
    taps, db = w.shape
    assert taps - 1 <= CONV_HALO
    ts = _tile(seq, 256)
    per_seq = seq // ts
    row = lambda v: v.reshape(1, db)
    vec = pl.BlockSpec((1, db), lambda bb, i: (0, 0))
    return pl.pallas_call(
        functools.partial(_conv_kernel, row_block=_tile(ts, 32)),
        name="conv_branch",
        grid=(bsz, per_seq),
        in_specs=[
            pl.BlockSpec((ts, db), lambda bb, i: (bb * per_seq + i, col_val)),
            pl.BlockSpec((ts, db), lambda bb, i: (bb * per_seq + i, col_gate)),
            pl.BlockSpec((taps, db), lambda bb, i: (0, 0)),
            vec, vec, vec,
        ],
        out_specs=pl.BlockSpec((ts, db), lambda bb, i: (bb * per_seq + i, 0)),
        out_shape=jax.ShapeDtypeStruct((bsz * seq, db), BF16),
        scratch_shapes=[
            pltpu.VMEM((ts + CONV_HALO, db), F32),
            pltpu.VMEM((SUBLANES - 1, ts + CONV_HALO - SUBLANES, db), F32),
        ],
        compiler_params=_params(("arbitrary", "arbitrary")),
    )(proj, proj, w, row(b), row(g), row(beta))


def _qk_prep_kernel(q_ref, k_ref, cos_ref, sa_ref, sb_ref, seg_ref, qg_ref, kg_ref, qo_ref, kt_ref, kn_ref):
    ts, db = q_ref.shape
    cos, sa, sb = cos_ref[...], sa_ref[...], sb_ref[...]

    def norm_rope(x, g):
        ms = jnp.dot((x * x).astype(BF16), seg_ref[...], preferred_element_type=F32)
        xn = x * lax.rsqrt(ms + EPS) * g
        return (xn * cos + pltpu.roll(xn, HEAD_W - ROPE_HALF, 1) * sa + pltpu.roll(xn, ROPE_HALF, 1) * sb)

    for h in range(db // HEAD_W):
        sl = slice(h * HEAD_W, (h + 1) * HEAD_W)
        q = norm_rope(q_ref[:, sl].astype(F32), qg_ref[...])
        qo_ref[:, sl] = (q * (LOG2_E / math.sqrt(HEAD_DIM))).astype(qo_ref.dtype)
        kn_ref[:, sl] = norm_rope(k_ref[:, sl].astype(F32), kg_ref[...])
    kt_ref[...] = kn_ref[...].T.astype(kt_ref.dtype)


def _qk_prep(proj, bsz, seq, col_q, col_k, db, cos_t, sa_t, sb_t, seg, qg, kg):
    ts = _tile(seq, 256)
    per_seq = seq // ts
    tab = pl.BlockSpec((ts, HEAD_W), lambda b, i: (b * per_seq + i, 0))
    vec = pl.BlockSpec((1, HEAD_W), lambda b, i: (0, 0))
    return pl.pallas_call(
        _qk_prep_kernel,
        name="qk_prep",
        grid=(bsz, per_seq),
        in_specs=[
            pl.BlockSpec((ts, db), lambda b, i: (b * per_seq + i, col_q)),
            pl.BlockSpec((ts, db), lambda b, i: (b * per_seq + i, col_k)),
            tab, tab, tab,
            pl.BlockSpec((HEAD_W, HEAD_W), lambda b, i: (0, 0)),
            vec, vec,
        ],
        out_specs=[
            pl.BlockSpec((ts, db), lambda b, i: (b * per_seq + i, 0)),
            pl.BlockSpec((None, db, ts), lambda b, i: (b, 0, i)),
        ],
        out_shape=[
            jax.ShapeDtypeStruct((bsz * seq, db), BF16),
            jax.ShapeDtypeStruct((bsz, db, seq), BF16),
        ],
        scratch_shapes=[pltpu.VMEM((ts, db), F32)],
        compiler_params=_params(("arbitrary", "arbitrary")),
    )(proj, proj, cos_t, sa_t, sb_t, seg, qg, kg)


def _flash_kernel(q_ref, kt_ref, v_ref, lq1_ref, lk1_ref, lq2_ref, lk2_ref, sg_ref, o_ref,
                  m_ref, l_ref, acc_ref, p_ref, s_ref, *, lambda_init, tk):
    tq = q_ref.shape[0]
    rows = 2 * tq
    i = pl.program_id(2)
    lane = lax.broadcasted_iota(jnp.int32, (tq, HEAD_W), 1)
    q = q_ref[...]
    zero = jnp.zeros_like(q)
    q2 = jnp.concatenate([jnp.where(lane < HEAD_DIM, q, zero), jnp.where(lane >= HEAD_DIM, q, zero)], axis=0)
    m_ref[...] = jnp.full(m_ref.shape, NEG_BIG, F32)
    l_ref[...] = jnp.zeros(l_ref.shape, F32)
    acc_ref[...] = jnp.zeros(acc_ref.shape, F32)
    p_ref[...] = jnp.zeros(p_ref.shape, p_ref.dtype)

    def pv(j):
        k0 = pl.multiple_of(j * tk, tk)
        return jnp.dot(p_ref[...], v_ref[pl.ds(k0, tk), :], preferred_element_type=F32)

    def scores(j):
        k0 = pl.multiple_of(j * tk, tk)
        return jnp.dot(q2, kt_ref[:, pl.ds(k0, tk)], preferred_element_type=F32)

    n_full = (i * tq) // tk
    n_all = ((i + 1) * tq + tk - 1) // tk
    s_ref[...] = scores(0)

    def step(j, masked):
        k0 = pl.multiple_of(j * tk, tk)
        s = s_ref[...]
        pv_prev = pv(jnp.maximum(j - 1, 0))
        s_ref[...] = scores(jnp.minimum(j + 1, n_all - 1))
        if masked:
            qpos = i * tq + lax.broadcasted_iota(jnp.int32, (rows, tk), 0) % tq
            kpos = k0 + lax.broadcasted_iota(jnp.int32, (rows, tk), 1)
            s = jnp.where(kpos <= qpos, s, NEG_BIG)
        cols = [s[:, cc * LANES:(cc + 1) * LANES] for cc in range(tk // LANES)]
        m_prev = m_ref[...]
        m_next = jnp.maximum(m_prev, jnp.max(functools.reduce(jnp.maximum, cols), axis=-1, keepdims=True))
        alpha = jnp.exp2(m_prev - m_next)
        ps = [jnp.exp2(col - m_next) for col in cols]
        l_ref[...] = alpha * l_ref[...] + functools.reduce(jnp.add, ps)
        acc_ref[...] = alpha * (acc_ref[...] + pv_prev)
        p_ref[...] = jnp.concatenate([x.astype(p_ref.dtype) for x in ps], axis=1)
        m_ref[...] = m_next

    def full_body(j, c):
        step(j, False)
        return c

    def diag_body(j, c):
        step(j, True)
        return c

    lax.fori_loop(0, n_full, full_body, 0)
    lax.fori_loop(n_full, n_all, diag_body, 0)

    lam = (jnp.exp(jnp.sum(lq1_ref[...] * lk1_ref[...], axis=-1, keepdims=True))
           - jnp.exp(jnp.sum(lq2_ref[...] * lk2_ref[...], axis=-1, keepdims=True)) + lambda_init)
    o_all = (acc_ref[...] + pv(n_all - 1)) / jnp.sum(l_ref[...], axis=-1, keepdims=True)
    o = o_all[0:tq, :] - lam * o_all[tq:rows, :]
    o = o * lax.rsqrt(jnp.mean(o * o, axis=-1, keepdims=True) + EPS) * sg_ref[...]
    o_ref[...] = (o * (1.0 - lambda_init)).astype(o_ref.dtype)


def _flash(qn, kt, proj, bsz, seq, col_v, lq1, lk1, lq2, lk2, sub_g, lambda_init):
    db = qn.shape[1]
    heads = db // HEAD_W
    tq = _tile(seq, 512)
    tk = _tile(seq, 512)
    per_seq = seq // tq
    vec64 =pl.BlockSpec((1, HEAD_DIM), lambda b, h, i: (0, 0))
    row64 = lambda v: v.reshape(1, HEAD_DIM)
    return pl.pallas_call(
        functools.partial(_flash_kernel, lambda_init=lambda_init, tk=tk),
        name="diff_attn",
        grid=(bsz, heads, per_seq),
        in_specs=[
            pl.BlockSpec((tq, HEAD_W), lambda b, h, i: (b * per_seq + i, h)),
            pl.BlockSpec((None, HEAD_W, seq), lambda b, h, i: (b, h, 0)),
            pl.BlockSpec((seq, HEAD_W), lambda b, h, i: (b, col_v * heads + h)),
            vec64, vec64, vec64, vec64,
            pl.BlockSpec((1, HEAD_W), lambda b, h, i: (0, 0)),
        ],
        out_specs=pl.BlockSpec((tq, HEAD_W), lambda b, h, i: (b * per_seq + i, h)),
        out_shape=jax.ShapeDtypeStruct((bsz * seq, db), BF16),
        scratch_shapes=[pltpu.VMEM((2 * tq, HEAD_W), F32)] * 3 + [
            pltpu.VMEM((2 * tq, tk), BF16),
            pltpu.VMEM((2 * tq, tk), F32),
        ],
        compiler_params=_params(("arbitrary", "arbitrary", "arbitrary")),
    )(qn, kt, proj, row64(lq1), row64(lk1), row64(lq2), row64(lk2), sub_g.reshape(1, HEAD_W))


def _rope_tables(positions):
    inv_freq = ROPE_THETA ** (-jnp.arange(0, ROPE_DIM, 2, dtype=F32) / ROPE_DIM)
    lane = jnp.arange(HEAD_W, dtype=jnp.int32) % HEAD_DIM
    freq = jnp.where(lane < ROPE_DIM, jnp.take(inv_freq, lane % ROPE_HALF), 0.0)
    ang = positions.astype(F32).reshape(-1, 1) * freq[None, :]
    sin = jnp.sin(ang)
    return jnp.cos(ang), jnp.where(lane < ROPE_HALF, -sin, 0.0), jnp.where(lane >= ROPE_HALF, sin, 0.0)


def _merge_kernel(h_ref, ya_ref, yb_ref, yc_ref, yd_ref, wa_ref, wb_ref, wc_ref, wd_ref,
                  bg_ref, wbr_ref, o_ref):
    h = h_ref[...]
    acc = None
    for bi, (y_ref, wg_ref) in enumerate(((ya_ref, wa_ref), (yb_ref, wb_ref), (yc_ref, wc_ref), (yd_ref, wd_ref))):
        gate = _sigmoid(jnp.dot(h, wg_ref[...], preferred_element_type=F32) + bg_ref[bi:bi + 1, :])
        term = gate * jnp.dot(y_ref[...], wbr_ref[bi], preferred_element_type=F32)
        acc = term if acc is None else acc + term
    o_ref[...] = acc.astype(o_ref.dtype)


def _merge(h, ys, w_in16, layer, n_mix, b_gate, w_branch16):
    n, d = h.shape
    db = ys[0].shape[1]
    n_branch = len(ys)
    tm = _tile(n, 1024)
    tn = _tile(d, 256)
    gate_blk = [(n_mix + b * d) // tn for b in range(n_branch)]
    once = pl.Buffered(1)
    wspecs = [pl.BlockSpec((None, d, tn), functools.partial(lambda j, i, base: (layer, 0, base + j), base=gb),
                           pipeline_mode=once) for gb in gate_blk]
    yspec = pl.BlockSpec((tm, db), lambda j, i: (i, 0))
    return pl.pallas_call(
        _merge_kernel,
        name="gated_merge",
        grid=(d // tn, n // tm),
        in_specs=[pl.BlockSpec((tm, d), lambda j, i: (i, 0))] + [yspec] * n_branch + wspecs + [
            pl.BlockSpec((n_branch, tn), lambda j, i: (0, j)),
            pl.BlockSpec((None, n_branch, db, tn), lambda j, i: (layer, 0, 0, j), pipeline_mode=once),
        ],
        out_specs=pl.BlockSpec((tm, tn), lambda j, i: (i, j)),
        out_shape=jax.ShapeDtypeStruct((n, d), BF16),
        compiler_params=_params(("arbitrary", "arbitrary")),
    )(h, *ys, *([w_in16] * n_branch), b_gate, w_branch16)


def _resid_matmul_kernel(a_ref, w_ref, x_ref, mod_ref, o_ref, *, gate_row):
    y = jnp.dot(a_ref[...], w_ref[...], preferred_element_type=F32)
    o_ref[...] = x_ref[...] + mod_ref[gate_row:gate_row + 1, :] * y


def _resid_matmul(a, w16, layer, x2, mod_l, seq, gate_row):
    n, k = a.shape
    d = w16.shape[2]
    tm = _tile(seq, 1024)
    tn = _tile(d, 1024)
    per_seq = seq // tm
    return pl.pallas_call(
        functools.partial(_resid_matmul_kernel, gate_row=gate_row),
        name="out_proj_resid",
        grid=(n // tm, d // tn),
        in_specs=[
            pl.BlockSpec((tm, k), lambda i, j: (i, 0)),
            pl.BlockSpec((None, k, tn), lambda i, j: (layer, 0, j)),
            pl.BlockSpec((tm, tn), lambda i, j: (i, j)),
            pl.BlockSpec((None, 6, tn), lambda i, j: (i // per_seq, 0, j)),
        ],
        out_specs=pl.BlockSpec((tm, tn), lambda i, j: (i, j)),
        out_shape=jax.ShapeDtypeStruct((n, d), F32),
        compiler_params=_params(("arbitrary", "arbitrary")),
    )(a, w16, x2, mod_l)


def _pack_halves(v):
    w = v.shape[1] // 2
    bits = lambda t: pltpu.bitcast(t.astype(BF16).astype(F32), jnp.uint32)
    return bits(v[:, w:]) | (bits(v[:, :w]) >> 16)


def _unpack_halves(p):
    return (pltpu.bitcast(p << 16, F32), pltpu.bitcast(p & jnp.uint32(0xFFFF0000), F32))


def _store_tiled_rows(ref3, val2):
    for s in range(ref3.shape[0]):
        ref3[s] = val2[:, s * LANES:(s + 1) * LANES]


def _load_tiled_rows(ref3):
    return jnp.concatenate([ref3[s] for s in range(ref3.shape[0])], axis=1)


def _router_kernel(x_ref, g_ref, mod_ref, wr_ref, br_ref, h_ref, r_ref, *, n_groups, per_group):
    h2 = _modnorm(x_ref[...], g_ref[...], mod_ref[...], 3, 4)
    _store_tiled_rows(h_ref, _pack_halves(h2))
    logits = jnp.dot(h2.astype(BF16), wr_ref[...], preferred_element_type=F32) + br_ref[...]
    lane = lax.broadcasted_iota(jnp.int32, logits.shape, 1)
    n_exp = n_groups * per_group
    is_g = lane < n_groups
    gl = jnp.where(is_g, logits, NEG_BIG)
    gmax = jnp.max(gl, axis=-1, keepdims=True)
    g_idx = jnp.min(jnp.where(gl == gmax, lane, LANES), axis=-1, keepdims=True)
    g_top = 1.0 / jnp.sum(jnp.where(is_g, jnp.exp(gl - gmax), 0.0), axis=-1, keepdims=True)
    lo = n_groups + g_idx * per_group
    in_grp = jnp.logical_and(lane >= lo, lane < lo + per_group)
    el = jnp.where(in_grp, logits, NEG_BIG)
    v1 = jnp.max(el, axis=-1, keepdims=True)
    i1 = jnp.min(jnp.where(el == v1, lane, LANES), axis=-1, keepdims=True)
    el2 = jnp.where(lane == i1, NEG_BIG, el)
    v2 = jnp.max(el2, axis=-1, keepdims=True)
    i2 = jnp.min(jnp.where(el2 == v2, lane, LANES), axis=-1, keepdims=True)
    e = jnp.exp(v2 - v1)
    w1 = g_top / (1.0 + e)
    w2 = g_top * e / (1.0 + e)
    del n_exp
    out = jnp.where(lane == 0, (i1 - n_groups).astype(F32),
                    jnp.where(lane == 1, (i2 - n_groups).astype(F32),
                              jnp.where(lane == 2, w1, jnp.where(lane == 3, w2, 0.0))))
    r_ref[...] = out


def _router(x2, g, mod_l, seq, w_router16, b_router, n_groups, per_group):
    n, d = x2.shape
    tm = _tile(seq, 256)
    per_seq = seq // tm
    return pl.pallas_call(
        functools.partial(_router_kernel, n_groups=n_groups, per_group=per_group),
        name="moe_router",
        grid=(n // tm,),
        in_specs=[
            pl.BlockSpec((tm, d), lambda i: (i, 0)),
            pl.BlockSpec((1, d), lambda i: (0, 0)),
            pl.BlockSpec((None, 6, d), lambda i: (i // per_seq, 0, 0)),
            pl.BlockSpec((d, LANES), lambda i: (0, 0)),
            pl.BlockSpec((1, LANES), lambda i: (0, 0)),
        ],
        out_specs=[
            pl.BlockSpec((d // (2 * LANES), tm, LANES), lambda i: (0, i, 0)),
            pl.BlockSpec((tm, LANES), lambda i: (i, 0)),
        ],
        out_shape=[
            jax.ShapeDtypeStruct((d // (2 * LANES), n, LANES), jnp.uint32),
            jax.ShapeDtypeStruct((n, LANES), F32),
        ],
        compiler_params=_params(("arbitrary",)),
    )(x2, g.reshape(1, d), mod_l, w_router16, b_router)


GATHER_UNROLL = 8
FFN_GATHER_SLOTS = 3


def _start_row_gather(idx_ref, base, n_rows, src_hbm, dst, sem):
    def body(g, c):
        r0 = pl.multiple_of(g * GATHER_UNROLL, GATHER_UNROLL)
        for k in range(GATHER_UNROLL):
            row = idx_ref[base + r0 + k]
            pltpu.make_async_copy(src_hbm.at[:, pl.ds(row, 1), :], dst.at[:, pl.ds(r0 + k, 1), :],
                                  sem).start(priority=k % 2)
        return c
    lax.fori_loop(0, n_rows // GATHER_UNROLL, body, 0)


def _wait_row_gather(src_hbm, dst, sem):
    pltpu.make_async_copy(src_hbm.at[:, pl.ds(0, dst.shape[1]), :], dst, sem).wait()


def _ffn_kernel(tile_exp_ref, n_tiles_ref, ring_ref, src_ref, h_hbm, w1_hbm, w3_hbm, w2_hbm, y_ref,
                xbuf, sem, w1buf, w3buf, w2buf, wsem, *, tm, layer):
    t = pl.program_id(0)
    n_tiles = n_tiles_ref[0]
    max_tiles = pl.num_programs(0)
    n_slots = xbuf.shape[0]
    ahead = n_slots - 1
    slot = t % n_slots

    def start_tile(tile):
        s = tile % n_slots
        _start_row_gather(src_ref, tile * tm, tm, h_hbm, xbuf.at[s], sem.at[s])

    for first in range(ahead):
        @pl.when(jnp.logical_and(t == 0, first < n_tiles))
        def _():
            start_tile(first)

    @pl.when(t + ahead < n_tiles)
    def _():
        start_tile(t + ahead)

    def weight_copies(expert, s):
        return [pltpu.make_async_copy(src.at[layer, expert], dst.at[s], wsem.at[s])
                for src, dst in ((w1_hbm, w1buf), (w3_hbm, w3buf), (w2_hbm, w2buf))]

    @pl.when(t < n_tiles)
    def _():
        ws = ring_ref[max_tiles + t]

        @pl.when(ring_ref[t] == 1)
        def _():
            @pl.when(t == 0)
            def _():
                for c in weight_copies(tile_exp_ref[0], ws):
                    c.start()

            next_expert = ring_ref[2 * max_tiles + t]

            @pl.when(next_expert >= 0)
            def _():
                for c in weight_copies(next_expert, 1 - ws):
                    c.start()

            for c in weight_copies(tile_exp_ref[t], ws):
                c.wait()

        _wait_row_gather(h_hbm, xbuf.at[slot], sem.at[slot])
        x_lo, x_hi = (v.astype(BF16) for v in _unpack_halves(_load_tiled_rows(xbuf.at[slot])))
        half = x_lo.shape[1]

        def up_proj(wbuf):
            return (jnp.dot(x_lo, wbuf[ws, 0:half, :], preferred_element_type=F32)
                    + jnp.dot(x_hi, wbuf[ws, half:2 * half, :], preferred_element_type=F32))

        a = up_proj(w1buf)
        b = up_proj(w3buf)
        hid = (a * _sigmoid(a) * b).astype(BF16)
        _store_tiled_rows(y_ref, _pack_halves(jnp.dot(hid, w2buf[ws], preferred_element_type=F32)))

    @pl.when(t >= n_tiles)
    def _():
        y_ref[...] = jnp.zeros_like(y_ref)


def _ffn(h2, tile_expert, n_tiles, ring, src_token, w1, w3, w2, layer, tm):
    d, de = w1.shape[2], w1.shape[3]
    n_chunks = h2.shape[0]
    max_tiles = tile_expert.shape[0]
    any_space = pl.BlockSpec(memory_space=pl.ANY)
    grid_spec = pltpu.PrefetchScalarGridSpec(
        num_scalar_prefetch=4,
        grid=(max_tiles,),
        in_specs=[any_space] * 4,
        out_specs=pl.BlockSpec((n_chunks, tm, LANES), lambda t, te, nt, rg, src: (0, t, 0)),
        scratch_shapes=[
            pltpu.VMEM((FFN_GATHER_SLOTS, n_chunks, tm, LANES), jnp.uint32),
            pltpu.SemaphoreType.DMA((FFN_GATHER_SLOTS,)),
            pltpu.VMEM((2, d, de), BF16),
            pltpu.VMEM((2, d, de), BF16),
            pltpu.VMEM((2, de, d), BF16),
            pltpu.SemaphoreType.DMA((2,)),
        ],
    )
    return pl.pallas_call(
        functools.partial(_ffn_kernel, tm=tm, layer=layer),
        name="moe_ffn",
        grid_spec=grid_spec,
        out_shape=jax.ShapeDtypeStruct((n_chunks, max_tiles * tm, LANES), jnp.uint32),
        compiler_params=_params(("arbitrary",), disable_bounds_checks=True),
    )(tile_expert, n_tiles, ring, src_token, h2, w1, w3, w2)


def _combine_kernel(pos_ref, y_hbm, x_ref, mod_ref, w_ref, o_ref, ybuf, sem, *, tc):
    t = pl.program_id(0)
    n_steps = pl.num_programs(0)
    slot = t % 2

    def start_step(step, s):
        _start_row_gather(pos_ref, step * (TOP_K * tc), TOP_K * tc, y_hbm, ybuf.at[s], sem.at[s])

    @pl.when(t == 0)
    def _():
        start_step(0, 0)

    @pl.when(t + 1 < n_steps)
    def _():
        start_step(t + 1, 1 - slot)

    _wait_row_gather(y_hbm, ybuf.at[slot], sem.at[slot])
    half = ybuf.shape[1] * ybuf.shape[3]
    chunk = _tile(tc, 2 * SUBLANES)
    for r0 in range(0, tc, chunk):
        w = w_ref[r0:r0 + chunk, :]
        first = _unpack_halves(_load_tiled_rows(ybuf.at[slot, :, r0:r0 + chunk, :]))
        second = _unpack_halves(_load_tiled_rows(ybuf.at[slot, :, tc + r0:tc + r0 + chunk, :]))
        for part in range(2):
            cols = slice(part * half, (part + 1) * half)
            moe = w[:, 2:3] * first[part] + w[:, 3:4] * second[part]
            o_ref[r0:r0 + chunk, cols] = x_ref[r0:r0 + chunk, cols] + mod_ref[5:6, cols] * moe


def _combine(pos_steps, y_sorted, x2, mod_l, route, seq, tc):
    n, d = x2.shape
    per_seq = seq // tc
    grid_spec = pltpu.PrefetchScalarGridSpec(
        num_scalar_prefetch=1,
        grid=(n // tc,),
        in_specs=[
            pl.BlockSpec(memory_space=pl.ANY),
            pl.BlockSpec((tc, d), lambda t, pos: (t, 0)),
            pl.BlockSpec((None, 6, d), lambda t, pos: (t // per_seq, 0, 0)),
            pl.BlockSpec((tc, LANES), lambda t, pos: (t, 0)),
        ],
        out_specs=pl.BlockSpec((tc, d), lambda t, pos: (t, 0)),
        scratch_shapes=[
            pltpu.VMEM((2, y_sorted.shape[0], TOP_K * tc, LANES), jnp.uint32),
            pltpu.SemaphoreType.DMA((2,)),
        ],
    )
    return pl.pallas_call(
        functools.partial(_combine_kernel, tc=tc),
        name="moe_combine",
        grid_spec=grid_spec,
        out_shape=jax.ShapeDtypeStruct((n, d), F32),
        compiler_params=_params(("arbitrary",), disable_bounds_checks=True),
    )(pos_steps, y_sorted, x2, mod_l, route)


def _moe_plan(route, n_exp, tm, tc):
    n = route.shape[0]
    eid = route[:, 0:TOP_K].astype(jnp.int32)
    flat_e = eid.reshape(-1)
    onehot = (flat_e[:, None] == jnp.arange(n_exp, dtype=jnp.int32)[None, :]).astype(jnp.int32)
    csum = jnp.cumsum(onehot, axis=0)
    rank = jnp.sum(csum * onehot, axis=1) - 1
    counts = csum[-1]
    tiles = (counts + tm - 1) // tm
    tile_end = jnp.cumsum(tiles)
    tile_start = tile_end - tiles
    n_tiles = tile_end[-1]
    pos = jnp.sum(onehot * (tile_start * tm)[None, :], axis=1) + rank
    max_tiles = (TOP_K * n) // tm + n_exp
    token = jnp.arange(TOP_K * n, dtype=jnp.int32) // TOP_K
    src_token = jnp.zeros((max_tiles * tm,), jnp.int32).at[pos].set(token)
    tix = jnp.minimum(jnp.arange(max_tiles, dtype=jnp.int32), n_tiles - 1)
    tile_expert = jnp.sum((tile_end[None, :] <= tix[:, None]).astype(jnp.int32), axis=1)
    t_idx = jnp.arange(max_tiles, dtype=jnp.int32)
    prev_expert = jnp.concatenate([jnp.full((1,), -1, jnp.int32), tile_expert[:-1]])
    first = jnp.logical_and(t_idx < n_tiles, tile_expert != prev_expert)
    w_slot = (jnp.cumsum(first.astype(jnp.int32)) - 1) % 2
    first_at_or_after = lax.cummin(jnp.where(first, t_idx, max_tiles), axis=0, reverse=True)
    next_first = jnp.concatenate([first_at_or_after[1:], jnp.full((1,), max_tiles, jnp.int32)])
    next_expert = jnp.where(next_first < max_tiles, tile_expert[jnp.minimum(next_first, max_tiles - 1)], -1)
    ring = jnp.stack([first.astype(jnp.int32), w_slot.astype(jnp.int32), next_expert.astype(jnp.int32)])
    pos_steps = pos.reshape(n // tc, tc, TOP_K).transpose(0, 2, 1).reshape(-1)
    return (tile_expert, n_tiles.reshape(1).astype(jnp.int32), ring.reshape(-1), src_token,
            pos_steps.astype(jnp.int32))


def kernel(x, c, positions, ada_w, ada_b, norm1_g, norm2_g, w_in, b_gate, lru_conv_w, lru_conv_b, lru_wr, lru_br, lru_wi, lru_bi, lru_lambda, pool_w, pool_b, pool_scale, q_norm_g, k_norm_g, lam_q1, lam_k1, lam_q2, lam_k2, subln_g, cv_dw_w, cv_dw_b, cv_ln_g, cv_ln_b, w_branch, w_out, router_g_w, router_g_b, router_e_w, router_e_b, moe_w1, moe_w3, moe_w2):
    bsz, seq, d = x.shape
    depth = ada_w.shape[0]
    db = lru_conv_w.shape[2]
    n_branch = w_branch.shape[1]
    n_mix = w_in.shape[2] - n_branch * d
    assert n_mix == 8 * db and q_norm_g.shape[1] == HEAD_DIM
    n_groups = router_g_w.shape[2]
    n_exp = router_e_w.shape[2]
    per_group = n_exp // n_groups
    assert n_groups + n_exp <= LANES
    n = bsz * seq
    col = {name: idx for idx, name in enumerate(("lru_x", "lru_gate", "pool", "q", "k", "v", "glu_val", "glu_gate"))}

    mod = _ada_mod(c, ada_w, ada_b).reshape(depth, bsz, 6, d)
    cos_t, sa_t, sb_t = _rope_tables(positions)
    seg = jnp.kron(jnp.eye(HEAD_W // HEAD_DIM, dtype=F32), jnp.full((HEAD_DIM, HEAD_DIM), 1.0 / HEAD_DIM, F32)).astype(BF16)
    dup = lambda g: jnp.concatenate([g, g]).reshape(1, HEAD_W)
    ffn_tm = _tile(TOP_K * n, 256)
    comb_tc = _tile(seq, 256)

    w_in16, w_branch16, w_out16 = w_in.astype(BF16), w_branch.astype(BF16), w_out.astype(BF16)
    moe_w1_16, moe_w3_16, moe_w2_16 = moe_w1.astype(BF16), moe_w3.astype(BF16), moe_w2.astype(BF16)

    x2 = x.reshape(n, d)
    for l in range(depth):
        mod_l = mod[l]
        h = _norm(x2, norm1_g[l], mod_l, seq, 0, 1)
        proj = _matmul(h, w_in16, l, n_mix)
        y_a = _lru_branch(proj, bsz, seq, col["lru_x"], col["lru_gate"], lru_conv_w[l], lru_conv_b[l],
                          lru_wr[l], lru_br[l], lru_wi[l], lru_bi[l], lru_lambda[l])
        y_b = _pool_branch(proj, bsz, seq, col["pool"], pool_w[l], pool_b[l], pool_scale[l])
        qn, kt = _qk_prep(proj, bsz, seq, col["q"], col["k"], db, cos_t, sa_t, sb_t, seg,
                          dup(q_norm_g[l]), dup(k_norm_g[l]))
        lambda_init = 0.8 - 0.6 * math.exp(-0.3 * l)
        y_c = _flash(qn, kt, proj, bsz, seq, col["v"], lam_q1[l], lam_k1[l], lam_q2[l], lam_k2[l],
                     subln_g[l], lambda_init)
        y_d = _conv_branch(proj, bsz, seq, col["glu_val"], col["glu_gate"], cv_dw_w[l], cv_dw_b[l],
                           cv_ln_g[l], cv_ln_b[l])
        merged = _merge(h, (y_a, y_b, y_c, y_d), w_in16, l, n_mix, b_gate[l], w_branch16)
        x2 = _resid_matmul(merged, w_out16, l, x2, mod_l, seq, 2)

        w_router = jnp.concatenate([router_g_w[l], router_e_w[l]], axis=1)
        w_router = jnp.pad(w_router, ((0, 0), (0, LANES - w_router.shape[1]))).astype(BF16)
        b_router = jnp.pad(jnp.concatenate([router_g_b[l], router_e_b[l]]), (0, LANES - n_groups - n_exp)).reshape(1, LANES)
        h2, route = _router(x2, norm2_g[l], mod_l, seq, w_router, b_router, n_groups, per_group)
        tile_expert, n_tiles, ring, src_token, pos_steps = _moe_plan(route, n_exp, ffn_tm, comb_tc)
        y_sorted = _ffn(h2, tile_expert, n_tiles, ring, src_token, moe_w1_16, moe_w3_16, moe_w2_16, l, ffn_tm)
        x2 = _combine(pos_steps, y_sorted, x2, mod_l, route, seq, comb_tc)
    return x2.reshape(bsz, seq, d)
```
